```python
import jax, jax.numpy as jnp
from jax import lax
import numpy as np

D_MODEL = 2048
BATCH = 4
SEQ = 2048
DEPTH = 4
DEC_BATCH = 8
DEC_SEQ = 8
PAST_LEN = 16384
PAGE_SIZE = 128

N_MIXERS = 3
N_LAYERS_A = (DEPTH + 2) // 3
N_LAYERS_B = (DEPTH + 1) // 3
N_LAYERS_C = DEPTH // 3
N_LAYERS_DENSE = (DEPTH + 1) // 2
N_LAYERS_MOE = DEPTH // 2
N_MOD = 6
EPS = 1e-6

A_K_HEADS = 16
A_V_HEADS = 32
A_HEAD_DIM = 128
A_KEY_DIM = A_K_HEADS * A_HEAD_DIM
A_VAL_DIM = A_V_HEADS * A_HEAD_DIM
A_CONV_DIM = 2 * A_KEY_DIM + A_VAL_DIM
A_IN_DIM = A_CONV_DIM + A_VAL_DIM + 2 * A_V_HEADS
A_CONV = 4
A_CHUNK = 64

B_GROUP = 16
B_GROUPS = D_MODEL // B_GROUP
B_STATE = 64

C_HEADS = 16
C_HEAD_DIM = D_MODEL // C_HEADS
C_WINDOWS = (128, 512, 2048)
C_DILATIONS = (1, 4, 16)
C_GROUPS = 3
C_QKV_DIM = C_GROUPS * 3 * C_HEADS * C_HEAD_DIM

F_DENSE = 5632
N_EXPERTS = 8
TOP_K = 2
F_EXPERT = 7168

kernel_name = 'hybrid_deltanet_s5_dilated_moe_step'


def _rmsnorm(x, g):
    xf = x.astype(jnp.float32)
    y = xf * lax.rsqrt(jnp.mean(xf * xf, axis=-1, keepdims=True) + EPS)
    return (y * g.astype(jnp.float32)).astype(x.dtype)


def _l2norm(x):
    return x * lax.rsqrt(jnp.sum(x * x, axis=-1, keepdims=True) + EPS)


def _adaln(c, w_mod, b_mod):
    m = (jax.nn.silu(c) @ w_mod + b_mod)[:, None, :]
    return jnp.split(m, N_MOD, axis=-1)


def _modulate(x, g, shift, scale):
    return _rmsnorm(x, g) * (1 + scale) + shift


def _swiglu(x, w_gu, w_down):
    g, u = jnp.split(x @ w_gu, 2, axis=-1)
    return (jax.nn.silu(g) * u) @ w_down


def _moe(x, router, w_gu, w_down):
    logits = (x @ router).astype(jnp.float32)
    top_v, top_i = lax.top_k(logits, TOP_K)
    gates = jax.nn.softmax(top_v, axis=-1)
    gate_full = jnp.sum(jax.nn.one_hot(top_i, N_EXPERTS, dtype=jnp.float32) * gates[..., None], axis=-2)
    out = jnp.zeros(x.shape, jnp.float32)
    for e in range(N_EXPERTS):
        out = out + gate_full[..., e:e + 1] * _swiglu(x, w_gu[e], w_down[e]).astype(jnp.float32)
    return out.astype(x.dtype)


def _short_conv(u, buf, w):
    T = u.shape[1]
    up = jnp.concatenate([buf.astype(u.dtype), u], axis=1)
    y = up[:, 0:T] * w[0]
    for j in range(1, A_CONV):
        y = y + up[:, j:j + T] * w[j]
    return jax.nn.silu(y), up[:, T:]


def _gated_delta(q, k, v, beta, g, S0):
    B, T, H, dk = k.shape
    dv = v.shape[-1]
    C = A_CHUNK
    n = -(-T // C)
    pad = n * C - T

    def chunks(a):
        a = jnp.pad(a, [(0, 0), (0, pad)] + [(0, 0)] * (a.ndim - 2))
        a = a.reshape((B, n, C) + a.shape[2:])
        return jnp.moveaxis(a, 3, 1)

    q, k, v, beta, g = (chunks(a) for a in (q, k, v, beta, g))
    gc = jnp.cumsum(g, axis=-1)
    causal = jnp.tril(jnp.ones((C, C), dtype=bool))
    strict = jnp.tril(jnp.ones((C, C), dtype=bool), -1)
    gdiff = gc[..., :, None] - gc[..., None, :]
    decay = jnp.where(causal, jnp.exp(jnp.where(causal, gdiff, 0.0)), 0.0)
    kb = k * beta[..., None]
    a_low = jnp.where(strict, jnp.einsum('bhnid,bhnjd->bhnij', kb, k) * decay, 0.0)
    rhs = jnp.concatenate([v * beta[..., None], kb * jnp.exp(gc)[..., None]], axis=-1)
    sol = lax.linalg.triangular_solve(a_low, rhs, left_side=True, lower=True, unit_diagonal=True)
    u, w = sol[..., :dv], sol[..., dv:]
    qk = jnp.einsum('bhnid,bhnjd->bhnij', q, k) * decay

    def step(S, xs):
        q_c, k_c, u_c, w_c, qk_c, gc_c = xs
        v_new = u_c - jnp.einsum('bhcd,bhde->bhce', w_c, S)
        o = (jnp.einsum('bhcd,bhde->bhce', q_c * jnp.exp(gc_c)[..., None], S)
             + jnp.einsum('bhij,bhje->bhie', qk_c, v_new))
        g_last = gc_c[..., -1:]
        S = (S * jnp.exp(g_last)[..., None]
             + jnp.einsum('bhcd,bhce->bhde', k_c * jnp.exp(g_last - gc_c)[..., None], v_new))
        return S, o

    xs = tuple(jnp.moveaxis(a, 2, 0) for a in (q, k, u, w, qk, gc))
    S, o = lax.scan(step, S0, xs)
    o = o.transpose(1, 0, 3, 2, 4).reshape(B, n * C, H, dv)[:, :T]
    return o, S


def _mixer_a(h, conv_buf, S0, w_in, conv_w, a_log, dt_bias, norm_g, w_out):
    B, T, _ = h.shape
    f32 = jnp.float32
    proj = h @ w_in
    qkv, z, b, a = jnp.split(proj, [A_CONV_DIM, A_CONV_DIM + A_VAL_DIM, A_CONV_DIM + A_VAL_DIM + A_V_HEADS], axis=-1)
    qkv, new_buf = _short_conv(qkv, conv_buf, conv_w)
    qkv = qkv.astype(f32)
    q, k, v = jnp.split(qkv, [A_KEY_DIM, 2 * A_KEY_DIM], axis=-1)
    rep = A_V_HEADS // A_K_HEADS
    q = jnp.repeat(_l2norm(q.reshape(B, T, A_K_HEADS, A_HEAD_DIM)) * (A_HEAD_DIM ** -0.5), rep, axis=2)
    k = jnp.repeat(_l2norm(k.reshape(B, T, A_K_HEADS, A_HEAD_DIM)), rep, axis=2)
    v = v.reshape(B, T, A_V_HEADS, A_HEAD_DIM)
    beta = jax.nn.sigmoid(b.astype(f32))
    g = -jnp.exp(a_log.astype(f32)) * jax.nn.softplus(a.astype(f32) + dt_bias.astype(f32))
    o, S = _gated_delta(q, k, v, beta, g, S0.astype(f32))
    o = _rmsnorm(o, norm_g) * jax.nn.silu(z.astype(f32).reshape(B, T, A_V_HEADS, A_HEAD_DIM))
    y = o.reshape(B, T, A_VAL_DIM).astype(h.dtype) @ w_out
    return y, new_buf, S.astype(S0.dtype)


def _complex_affine_combine(e1, e2):
    a1r, a1i, b1r, b1i = e1
    a2r, a2i, b2r, b2i = e2
    return (a2r * a1r - a2i * a1i, a2r * a1i + a2i * a1r,
            a2r * b1r - a2i * b1i + b2r, a2r * b1i + a2i * b1r + b2i)


def _mixer_b(h, s_re, s_im, lam_re, lam_im, log_dt, b_re, b_im, c_re, c_im, d_skip, w_glu):
    B, T, _ = h.shape
    f32 = jnp.float32
    lam_re, lam_im, b_re, b_im, c_re, c_im = (a.astype(f32) for a in (lam_re, lam_im, b_re, b_im, c_re, c_im))
    u = h.astype(f32)
    ug = u.reshape(B, T, B_GROUPS, B_GROUP)
    dt = jnp.exp(log_dt.astype(f32))[:, None]
    mag = jnp.exp(lam_re * dt)
    ab_re, ab_im = mag * jnp.cos(lam_im * dt), mag * jnp.sin(lam_im * dt)
    den = lam_re * lam_re + lam_im * lam_im
    f_re = ((ab_re - 1.0) * lam_re + ab_im * lam_im) / den
    f_im = (ab_im * lam_re - (ab_re - 1.0) * lam_im) / den
    bb_re = f_re[..., None] * b_re - f_im[..., None] * b_im
    bb_im = f_re[..., None] * b_im + f_im[..., None] * b_re
    bu_re = jnp.einsum('btgc,gpc->btgp', ug, bb_re)
    bu_im = jnp.einsum('btgc,gpc->btgp', ug, bb_im)
    s_re, s_im = s_re.astype(f32), s_im.astype(f32)
    bu_re = bu_re.at[:, 0].add(ab_re * s_re - ab_im * s_im)
    bu_im = bu_im.at[:, 0].add(ab_re * s_im + ab_im * s_re)
    a_re = jnp.broadcast_to(ab_re, bu_re.shape)
    a_im = jnp.broadcast_to(ab_im, bu_im.shape)
    _, _, x_re, x_im = lax.associative_scan(_complex_affine_combine, (a_re, a_im, bu_re, bu_im), axis=1)
    y = jnp.einsum('btgp,gcp->btgc', x_re, c_re) - jnp.einsum('btgp,gcp->btgc', x_im, c_im)
    y = y.reshape(B, T, D_MODEL) + d_skip.astype(f32) * u
    y = jax.nn.gelu(y).astype(h.dtype)
    ya, yb = jnp.split(y @ w_glu, 2, axis=-1)
    return ya * jax.nn.sigmoid(yb), x_re[:, -1], x_im[:, -1]


def _attend(s, v, spec):
    m = jnp.max(s, axis=-1, keepdims=True)
    p = jnp.exp(s - m)
    l = jnp.sum(p, axis=-1)
    o = jnp.einsum(spec, p, v) / jnp.swapaxes(l, -1, -2)[..., None]
    lse = jnp.swapaxes(m[..., 0] + jnp.log(l), -1, -2)
    return o, lse


def _dilated_prompt(q, k, v, dil, nw):
    B, T, H, hd = q.shape
    L = T // dil
    nb = -(-L // nw)
    lp = nb * nw
    z = B * dil

    def by_residue(a):
        return a.reshape(B, L, dil, H, hd).transpose(0, 2, 1, 3, 4).reshape(z, L, H, hd)

    qs = jnp.pad(by_residue(q), ((0, 0), (0, lp - L), (0, 0), (0, 0))).reshape(z, nb, nw, H, hd)

    def key_blocks(a):
        ap = jnp.pad(by_residue(a), ((0, 0), (nw, lp - L), (0, 0), (0, 0)))
        return jnp.concatenate([ap[:, :lp].reshape(z, nb, nw, H, hd),
                                ap[:, nw:].reshape(z, nb, nw, H, hd)], axis=2)

    kb, vb = key_blocks(k), key_blocks(v)
    qi = jnp.arange(nw)[:, None]
    ks = jnp.arange(2 * nw)[None, :]
    dist = qi + nw - ks
    first = (jnp.arange(nb) * nw - nw)[:, None, None] + ks[None]
    mask = (dist >= 0) & (dist <= nw) & (first >= 0)
    s = jnp.einsum('znqhd,znkhd->znhqk', qs, kb) * (hd ** -0.5)
    s = jnp.where(mask[None, :, None], s, -jnp.inf)
    o, lse = _attend(s, vb, 'znhqk,znkhd->znqhd')
    o = o.reshape(z, lp, H, hd)[:, :L].reshape(B, dil, L, H, hd).transpose(0, 2, 1, 3, 4).reshape(B, T, H, hd)
    lse = lse.reshape(z, lp, H)[:, :L].reshape(B, dil, L, H).transpose(0, 2, 1, 3).reshape(B, T, H)
    return o, lse


def _dilated_sample(q, k, v, k_buf, v_buf, dil, nw):
    Tn = q.shape[1]
    lb = k_buf.shape[1]
    kc = jnp.concatenate([k_buf.astype(k.dtype), k], axis=1)
    vc = jnp.concatenate([v_buf.astype(v.dtype), v], axis=1)
    idx = lb + jnp.arange(Tn)[:, None] - dil * jnp.arange(nw + 1)[None, :]
    valid = idx >= 0
    idx = jnp.maximum(idx, 0)
    kg = jnp.take(kc, idx, axis=1)
    vg = jnp.take(vc, idx, axis=1)
    s = jnp.einsum('bqhd,bqkhd->bhqk', q, kg) * (q.shape[-1] ** -0.5)
    s = jnp.where(valid[None, None], s, -jnp.inf)
    o, lse = _attend(s, vg, 'bhqk,bqkhd->bqhd')
    return o, lse, kc[:, -lb:], vc[:, -lb:]


def _mixer_c(h, bufs, w_qkv, q_norm_g, k_norm_g, w_out):
    B, T, _ = h.shape
    f32 = jnp.float32
    qkv = (h @ w_qkv).astype(f32).reshape(B, T, C_GROUPS, 3, C_HEADS, C_HEAD_DIM)
    q = _rmsnorm(qkv[:, :, :, 0], q_norm_g[:, None, :])
    k = _rmsnorm(qkv[:, :, :, 1], k_norm_g[:, None, :])
    v = qkv[:, :, :, 2]
    outs, lses, new_bufs = [], [], []
    for gi in range(C_GROUPS):
        dil = C_DILATIONS[gi]
        nw = C_WINDOWS[gi] // dil
        qg, kg, vg = q[:, :, gi], k[:, :, gi], v[:, :, gi]
        if bufs is None:
            o, lse = _dilated_prompt(qg, kg, vg, dil, nw)
            lb = min(C_WINDOWS[gi], T)
            nk, nv = kg[:, T - lb:], vg[:, T - lb:]
        else:
            o, lse, nk, nv = _dilated_sample(qg, kg, vg, bufs[gi][0], bufs[gi][1], dil, nw)
        outs.append(o)
        lses.append(lse)
        new_bufs.append((nk.astype(h.dtype), nv.astype(h.dtype)))
    wts = jax.nn.softmax(jnp.stack(lses), axis=0)
    o = jnp.sum(wts[..., None] * jnp.stack(outs), axis=0)
    y = o.reshape(B, T, D_MODEL).astype(h.dtype) @ w_out
    return y, new_bufs


def _trunk(x, c, st, p):
    B = x.shape[0]
    a_conv, a_rec, b_re, b_im = [], [], [], []
    c_k = [[] for _ in range(C_GROUPS)]
    c_v = [[] for _ in range(C_GROUPS)]
    for i in range(DEPTH):
        sh1, sc1, gt1, sh2, sc2, gt2 = _adaln(c, p['w_mod'][i], p['b_mod'][i])
        h = _modulate(x, p['g_mix'][i], sh1, sc1)
        kind, slot = i % N_MIXERS, i // N_MIXERS
        if kind == 0:
            if st is None:
                buf0 = jnp.zeros((B, A_CONV - 1, A_CONV_DIM), x.dtype)
                s0 = jnp.zeros((B, A_V_HEADS, A_HEAD_DIM, A_HEAD_DIM), jnp.float32)
            else:
                buf0, s0 = st['a_conv'][slot], st['a_rec'][slot]
            y, nbuf, ns = _mixer_a(h, buf0, s0, p['a_w_in'][slot], p['a_conv_w'][slot], p['a_log'][slot],
                                   p['a_dt_bias'][slot], p['a_norm_g'][slot], p['a_w_out'][slot])
            a_conv.append(nbuf)
            a_rec.append(ns)
        elif kind == 1:
            if st is None:
                r0 = jnp.zeros((B, B_GROUPS, B_STATE), jnp.float32)
                m0 = jnp.zeros((B, B_GROUPS, B_STATE), jnp.float32)
            else:
                r0, m0 = st['b_re'][slot], st['b_im'][slot]
            y, nr, ni = _mixer_b(h, r0, m0, p['b_lambda_re'][slot], p['b_lambda_im'][slot], p['b_log_dt'][slot],
                                 p['b_B_re'][slot], p['b_B_im'][slot], p['b_C_re'][slot], p['b_C_im'][slot],
                                 p['b_D'][slot], p['b_w_glu'][slot])
            b_re.append(nr)
            b_im.append(ni)
        else:
            bufs = None if st is None else [(st['c_k'][g][slot], st['c_v'][g][slot]) for g in range(C_GROUPS)]
            y, nbs = _mixer_c(h, bufs, p['c_w_qkv'][slot], p['c_q_norm_g'][slot], p['c_k_norm_g'][slot],
                              p['c_w_out'][slot])
            for g in range(C_GROUPS):
                c_k[g].append(nbs[g][0])
                c_v[g].append(nbs[g][1])
        x = x + gt1 * y
        h = _modulate(x, p['g_ffn'][i], sh2, sc2)
        if i % 2 == 0:
            y = _swiglu(h, p['f_w_gate_up'][i // 2], p['f_w_down'][i // 2])
        else:
            y = _moe(h, p['m_router'][i // 2], p['m_w_gate_up'][i // 2], p['m_w_down'][i // 2])
        x = x + gt2 * y
    stk = lambda lst: jnp.stack(lst, axis=0)
    states = (stk(a_conv), stk(a_rec), stk(b_re), stk(b_im), stk(c_k[0]), stk(c_v[0]),
              stk(c_k[1]), stk(c_v[1]), stk(c_k[2]), stk(c_v[2]))
    return x, states


def setup_inputs(seed: int = 0) -> dict:
    key = jax.random.key(seed)
    keys = iter(jax.random.split(key, 64))
    f32 = jnp.float32

    def nrm(shape, scale=1.0):
        return jax.random.normal(next(keys), shape, f32) * scale

    def unif(shape, lo, hi):
        return jax.random.uniform(next(keys), shape, f32, lo, hi)

    D = D_MODEL
    lb = [min(w, PAST_LEN) for w in C_WINDOWS]
    dt_a = jnp.exp(unif((N_LAYERS_A, A_V_HEADS), float(np.log(1e-3)), float(np.log(1e-1))))
    return {
        'x_prompt': nrm((BATCH, SEQ, D)),
        'x_sample': nrm((DEC_BATCH, DEC_SEQ, D)),
        'c_prompt': nrm((BATCH, D)),
        'c_sample': nrm((DEC_BATCH, D)),
        'state_a_conv': nrm((N_LAYERS_A, DEC_BATCH, A_CONV - 1, A_CONV_DIM)),
        'state_a_rec': nrm((N_LAYERS_A, DEC_BATCH, A_V_HEADS, A_HEAD_DIM, A_HEAD_DIM), 0.1),
        'state_b_re': nrm((N_LAYERS_B, DEC_BATCH, B_GROUPS, B_STATE), 0.3),
        'state_b_im': nrm((N_LAYERS_B, DEC_BATCH, B_GROUPS, B_STATE), 0.3),
        'cache_c_k0': nrm((N_LAYERS_C, DEC_BATCH, lb[0], C_HEADS, C_HEAD_DIM)),
        'cache_c_v0': nrm((N_LAYERS_C, DEC_BATCH, lb[0], C_HEADS, C_HEAD_DIM)),
        'cache_c_k1': nrm((N_LAYERS_C, DEC_BATCH, lb[1], C_HEADS, C_HEAD_DIM)),
        'cache_c_v1': nrm((N_LAYERS_C, DEC_BATCH, lb[1], C_HEADS, C_HEAD_DIM)),
        'cache_c_k2': nrm((N_LAYERS_C, DEC_BATCH, lb[2], C_HEADS, C_HEAD_DIM)),
        'cache_c_v2': nrm((N_LAYERS_C, DEC_BATCH, lb[2], C_HEADS, C_HEAD_DIM)),
        'g_mix': 1.0 + nrm((DEPTH, D), 0.1),
        'g_ffn': 1.0 + nrm((DEPTH, D), 0.1),
        'w_mod': nrm((DEPTH, D, N_MOD * D), 0.5 * D ** -0.5),
        'b_mod': nrm((DEPTH, N_MOD * D), 0.01),
        'a_w_in': nrm((N_LAYERS_A, D, A_IN_DIM), D ** -0.5),
        'a_conv_w': nrm((N_LAYERS_A, A_CONV, A_CONV_DIM), A_CONV ** -0.5),
        'a_log': jnp.log(unif((N_LAYERS_A, A_V_HEADS), 1.0, 16.0)),
        'a_dt_bias': dt_a + jnp.log(-jnp.expm1(-dt_a)),
        'a_norm_g': 1.0 + nrm((N_LAYERS_A, A_HEAD_DIM), 0.1),
        'a_w_out': nrm((N_LAYERS_A, A_VAL_DIM, D), A_VAL_DIM ** -0.5),
        'b_lambda_re': -0.5 + nrm((N_LAYERS_B, B_GROUPS, B_STATE), 0.01),
        'b_lambda_im': jnp.pi * jnp.arange(B_STATE, dtype=f32) + nrm((N_LAYERS_B, B_GROUPS, B_STATE), 0.01),
        'b_log_dt': unif((N_LAYERS_B, B_GROUPS), float(np.log(1e-3)), float(np.log(1e-1))),
        'b_B_re': nrm((N_LAYERS_B, B_GROUPS, B_STATE, B_GROUP), (2 * B_GROUP) ** -0.5),
        'b_B_im': nrm((N_LAYERS_B, B_GROUPS, B_STATE, B_GROUP), (2 * B_GROUP) ** -0.5),
        'b_C_re': nrm((N_LAYERS_B, B_GROUPS, B_GROUP, B_STATE), (2 * B_STATE) ** -0.5),
        'b_C_im': nrm((N_LAYERS_B, B_GROUPS, B_GROUP, B_STATE), (2 * B_STATE) ** -0.5),
        'b_D': nrm((N_LAYERS_B, D)),
        'b_w_glu': nrm((N_LAYERS_B, D, 2 * D), D ** -0.5),
        'c_w_qkv': nrm((N_LAYERS_C, D, C_QKV_DIM), D ** -0.5),
        'c_q_norm_g': 1.0 + nrm((N_LAYERS_C, C_GROUPS, C_HEAD_DIM), 0.1),
        'c_k_norm_g': 1.0 + nrm((N_LAYERS_C, C_GROUPS, C_HEAD_DIM), 0.1),
        'c_w_out': nrm((N_LAYERS_C, C_HEADS * C_HEAD_DIM, D), D ** -0.5),
        'f_w_gate_up': nrm((N_LAYERS_DENSE, D, 2 * F_DENSE), D ** -0.5),
        'f_w_down': nrm((N_LAYERS_DENSE, F_DENSE, D), F_DENSE ** -0.5),
        'm_router': nrm((N_LAYERS_MOE, D, N_EXPERTS), D ** -0.5),
        'm_w_gate_up': nrm((N_LAYERS_MOE, N_EXPERTS, D, 2 * F_EXPERT), D ** -0.5),
        'm_w_down': nrm((N_LAYERS_MOE, N_EXPERTS, F_EXPERT, D), F_EXPERT ** -0.5),
    }


def reference(x_prompt, x_sample, c_prompt, c_sample, state_a_conv, state_a_rec, state_b_re, state_b_im,
              cache_c_k0, cache_c_v0, cache_c_k1, cache_c_v1, cache_c_k2, cache_c_v2,
              g_mix, g_ffn, w_mod, b_mod, a_w_in, a_conv_w, a_log, a_dt_bias, a_norm_g, a_w_out,
              b_lambda_re, b_lambda_im, b_log_dt, b_B_re, b_B_im, b_C_re, b_C_im, b_D, b_w_glu,
              c_w_qkv, c_q_norm_g, c_k_norm_g, c_w_out, f_w_gate_up, f_w_down,
              m_router, m_w_gate_up, m_w_down):
    p = dict(g_mix=g_mix, g_ffn=g_ffn, w_mod=w_mod, b_mod=b_mod,
             a_w_in=a_w_in, a_conv_w=a_conv_w, a_log=a_log, a_dt_bias=a_dt_bias, a_norm_g=a_norm_g, a_w_out=a_w_out,
             b_lambda_re=b_lambda_re, b_lambda_im=b_lambda_im, b_log_dt=b_log_dt, b_B_re=b_B_re, b_B_im=b_B_im,
             b_C_re=b_C_re, b_C_im=b_C_im, b_D=b_D, b_w_glu=b_w_glu,
             c_w_qkv=c_w_qkv, c_q_norm_g=c_q_norm_g, c_k_norm_g=c_k_norm_g, c_w_out=c_w_out,
             f_w_gate_up=f_w_gate_up, f_w_down=f_w_down,
             m_router=m_router, m_w_gate_up=m_w_gate_up, m_w_down=m_w_down)
    st = dict(a_conv=state_a_conv, a_rec=state_a_rec, b_re=state_b_re, b_im=state_b_im,
              c_k=(cache_c_k0, cache_c_k1, cache_c_k2), c_v=(cache_c_v0, cache_c_v1, cache_c_v2))
    y_prompt, (pa_conv, pa_rec, pb_re, pb_im, pc_k0, pc_v0, pc_k1, pc_v1, pc_k2, pc_v2) = _trunk(x_prompt, c_prompt, None, p)
    y_sample, (sa_conv, sa_rec, sb_re, sb_im, sc_k0, sc_v0, sc_k1, sc_v1, sc_k2, sc_v2) = _trunk(x_sample, c_sample, st, p)
    return (y_prompt, y_sample, pa_conv, pa_rec, pb_re, pb_im, pc_k0, pc_v0, pc_k1, pc_v1, pc_k2, pc_v2,
            sa_conv, sa_rec, sb_re, sb_im, sc_k0, sc_v0, sc_k1, sc_v1, sc_k2, sc_v2)
```

```python
import functools

import jax
import jax.numpy as jnp
from jax import lax
from jax.experimental import pallas as pl
from jax.experimental.pallas import tpu as pltpu

D_MODEL = 2048
DEPTH = 4
N_MIXERS = 3
N_MOD = 6
EPS = 1e-6

A_K_HEADS = 16
A_V_HEADS = 32
A_HEAD_DIM = 128
A_KEY_DIM = A_K_HEADS * A_HEAD_DIM
A_VAL_DIM = A_V_HEADS * A_HEAD_DIM
A_CONV_DIM = 2 * A_KEY_DIM + A_VAL_DIM
A_CONV = 4
A_CHUNK = 64

B_GROUP = 16
B_GROUPS = D_MODEL // B_GROUP
B_STATE = 64

C_HEADS = 16
C_HEAD_DIM = D_MODEL // C_HEADS
C_WINDOWS = (128, 512, 2048)
C_DILATIONS = (1, 4, 16)
C_GROUPS = 3

F_DENSE = 5632
N_EXPERTS = 8
TOP_K = 2
F_EXPERT = 7168

LANE = 128
VMEM_LIMIT = 56 * 1024 * 1024
BF16 = jnp.bfloat16
F32 = jnp.float32


def _params(n_grid):
    return pltpu.CompilerParams(dimension_semantics=("arbitrary",) * n_grid,
                                vmem_limit_bytes=VMEM_LIMIT)


def _cast_weights_on_first_row_tile(w_refs, wb_refs):
    @pl.when(pl.program_id(1) == 0)
    def _():
        for w_ref, wb_ref in zip(w_refs, wb_refs):
            wb_ref[...] = w_ref[...].astype(BF16)


def _dot(x, wb_ref):
    return jnp.dot(x, wb_ref[...], preferred_element_type=F32)


def _k_plain(x_ref, w_ref, o_ref, wb_ref):
    _cast_weights_on_first_row_tile((w_ref,), (wb_ref,))
    o_ref[...] = _dot(x_ref[...], wb_ref).astype(o_ref.dtype)


def _k_resid(x_ref, w_ref, r_ref, g_ref, o_ref, wb_ref):
    _cast_weights_on_first_row_tile((w_ref,), (wb_ref,))
    o_ref[...] = r_ref[...] + g_ref[...] * _dot(x_ref[...], wb_ref)


def _k_swiglu(x_ref, wg_ref, wu_ref, o_ref, wgb_ref, wub_ref):
    _cast_weights_on_first_row_tile((wg_ref, wu_ref), (wgb_ref, wub_ref))
    x = x_ref[...]
    g = _dot(x, wgb_ref)
    u = _dot(x, wub_ref)
    o_ref[...] = (g * jax.nn.sigmoid(g) * u).astype(o_ref.dtype)


def _k_glu_resid(x_ref, wa_ref, wb_ref, r_ref, g_ref, o_ref, wab_ref, wbb_ref):
    _cast_weights_on_first_row_tile((wa_ref, wb_ref), (wab_ref, wbb_ref))
    x = x_ref[...]
    ya = _dot(x, wab_ref)
    yb = _dot(x, wbb_ref)
    o_ref[...] = r_ref[...] + g_ref[...] * (ya * jax.nn.sigmoid(yb))


def _linear(x, w, widx, *, n_out, tm, tn, mode="plain", col_off=0, col_off2=0,
            res=None, gate=None, out_dtype=F32):
    M, K = x.shape
    assert M % tm == 0 and n_out % tn == 0
    ni, nj = M // tm, n_out // tn
    lead = (None,) * len(widx)

    def wspec(off):
        return pl.BlockSpec(lead + (K, tn), lambda j, i: tuple(widx) + (0, j + off))

    xspec = pl.BlockSpec((tm, K), lambda j, i: (i, 0))
    ospec = pl.BlockSpec((tm, tn), lambda j, i: (i, j))
    two = mode in ("swiglu", "glu_resid")
    in_specs = [xspec, wspec(col_off)] + ([wspec(col_off2)] if two else [])
    args = [x, w] + ([w] if two else [])
    if mode in ("resid", "glu_resid"):
        tiles_per_gate = ni // gate.shape[0]
        in_specs += [ospec,
                     pl.BlockSpec((None, gate.shape[1], tn),
                                  lambda j, i: (i // tiles_per_gate, 0, j))]
        args += [res, gate]
    body = {"plain": _k_plain, "resid": _k_resid, "swiglu": _k_swiglu,
            "glu_resid": _k_glu_resid}[mode]
    return pl.pallas_call(
        body,
        grid=(nj, ni),
        in_specs=in_specs,
        out_specs=ospec,
        out_shape=jax.ShapeDtypeStruct((M, n_out), out_dtype),
        scratch_shapes=[pltpu.VMEM((K, tn), BF16)] * (2 if two else 1),
        compiler_params=_params(2),
        name="linear_" + mode,
    )(*args)


def _cast_weights_on_expert_change(te_ref, w_refs, wb_refs):
    t = pl.program_id(1)
    prev = te_ref[jnp.maximum(t - 1, 0)]

    @pl.when((t == 0) | (te_ref[t] != prev))
    def _():
        for w_ref, wb_ref in zip(w_refs, wb_refs):
            wb_ref[...] = w_ref[...].astype(BF16)


def _k_moe_up(te_ref, tv_ref, x_ref, wg_ref, wu_ref, o_ref, wgb_ref, wub_ref):
    _cast_weights_on_expert_change(te_ref, (wg_ref, wu_ref), (wgb_ref, wub_ref))
    valid = tv_ref[pl.program_id(1)] > 0

    @pl.when(valid)
    def _():
        x = x_ref[...]
        g = _dot(x, wgb_ref)
        u = _dot(x, wub_ref)
        o_ref[...] = (g * jax.nn.sigmoid(g) * u).astype(o_ref.dtype)

    @pl.when(jnp.logical_not(valid))
    def _():
        o_ref[...] = jnp.zeros_like(o_ref)


def _k_moe_down(te_ref, tv_ref, x_ref, w_ref, o_ref, wb_ref):
    _cast_weights_on_expert_change(te_ref, (w_ref,), (wb_ref,))
    valid = tv_ref[pl.program_id(1)] > 0

    @pl.when(valid)
    def _():
        o_ref[...] = _dot(x_ref[...], wb_ref)

    @pl.when(jnp.logical_not(valid))
    def _():
        o_ref[...] = jnp.zeros_like(o_ref)


def _moe_linear(x, w, layer, tile_expert, tile_valid, *, n_out, tm, tn, up):
    Mp, K = x.shape
    nt, nj = Mp // tm, n_out // tn

    def wspec(off):
        return pl.BlockSpec((None, None, K, tn),
                            lambda j, t, te, tv: (layer, te[t], 0, j + off))

    xspec = pl.BlockSpec((tm, K), lambda j, t, te, tv: (t, 0))
    ospec = pl.BlockSpec((tm, tn), lambda j, t, te, tv: (t, j))
    in_specs = [xspec, wspec(0)] + ([wspec(nj)] if up else [])
    args = [x, w] + ([w] if up else [])
    return pl.pallas_call(
        _k_moe_up if up else _k_moe_down,
        grid_spec=pltpu.PrefetchScalarGridSpec(
            num_scalar_prefetch=2,
            grid=(nj, nt),
            in_specs=in_specs,
            out_specs=ospec,
            scratch_shapes=[pltpu.VMEM((K, tn), BF16)] * (2 if up else 1)),
        out_shape=jax.ShapeDtypeStruct((Mp, n_out), BF16 if up else F32),
        compiler_params=_params(2),
        name="moe_up" if up else "moe_down",
    )(tile_expert, tile_valid, *args)


def _k_adaln(c_ref, w_ref, b_ref, o_ref):
    c = c_ref[...]
    a = (c * jax.nn.sigmoid(c)).astype(BF16)
    o_ref[...] = jnp.dot(a, w_ref[...].astype(BF16), preferred_element_type=F32) + b_ref[...]


def _adaln(c_all, w_mod, b_mod):
    R = c_all.shape[0]
    tn = 1024
    n = N_MOD * D_MODEL
    return pl.pallas_call(
        _k_adaln,
        grid=(DEPTH, n // tn),
        in_specs=[pl.BlockSpec((R, D_MODEL), lambda l, j: (0, 0)),
                  pl.BlockSpec((None, D_MODEL, tn), lambda l, j: (l, 0, j)),
                  pl.BlockSpec((None, 1, tn), lambda l, j: (l, 0, j))],
        out_specs=pl.BlockSpec((None, R, tn), lambda l, j: (l, 0, j)),
        out_shape=jax.ShapeDtypeStruct((DEPTH, R, n), F32),
        compiler_params=_params(2),
        name="adaln",
    )(c_all, w_mod, b_mod.reshape(DEPTH, 1, n))


def _modulated(x_ref, g_ref, sh_ref, sc_ref):
    x = x_ref[...]
    y = x * lax.rsqrt(jnp.mean(x * x, axis=-1, keepdims=True) + EPS)
    return (y * g_ref[...]) * (1.0 + sc_ref[...]) + sh_ref[...]


def _k_modulate(x_ref, g_ref, sh_ref, sc_ref, hb_ref):
    hb_ref[...] = _modulated(x_ref, g_ref, sh_ref, sc_ref).astype(BF16)


def _k_modulate_f32(x_ref, g_ref, sh_ref, sc_ref, hb_ref, hf_ref):
    h = _modulated(x_ref, g_ref, sh_ref, sc_ref)
    hb_ref[...] = h.astype(BF16)
    hf_ref[...] = h


def _k_modulate_router(x_ref, g_ref, sh_ref, sc_ref, r_ref, hb_ref, lg_ref):
    h = _modulated(x_ref, g_ref, sh_ref, sc_ref)
    hb = h.astype(BF16)
    hb_ref[...] = hb
    lg_ref[...] = jnp.dot(hb, r_ref[...].astype(BF16), preferred_element_type=F32)


def _modulate(x, g, shift, scale, *, tm, extra=None, router=None):
    M = x.shape[0]
    ni = M // tm
    tiles_per_vec = ni // shift.shape[0]
    xspec = pl.BlockSpec((tm, D_MODEL), lambda i: (i, 0))
    vspec = pl.BlockSpec((None, shift.shape[1], D_MODEL), lambda i: (i // tiles_per_vec, 0, 0))
    in_specs = [xspec, pl.BlockSpec((1, D_MODEL), lambda i: (0, 0)), vspec, vspec]
    args = [x, g.reshape(1, D_MODEL), shift, scale]
    out_specs = [xspec]
    out_shape = [jax.ShapeDtypeStruct((M, D_MODEL), BF16)]
    body = _k_modulate
    if extra == "f32":
        body = _k_modulate_f32
        out_specs.append(xspec)
        out_shape.append(jax.ShapeDtypeStruct((M, D_MODEL), F32))
    elif extra == "router":
        body = _k_modulate_router
        in_specs.append(pl.BlockSpec((D_MODEL, LANE), lambda i: (0, 0)))
        args.append(router)
        out_specs.append(pl.BlockSpec((tm, LANE), lambda i: (i, 0)))
        out_shape.append(jax.ShapeDtypeStruct((M, LANE), F32))
    return pl.pallas_call(
        body, grid=(ni,), in_specs=in_specs, out_specs=out_specs, out_shape=out_shape,
        compiler_params=_params(1), name="modulate",
    )(*args)


def _rmsnorm(x, g):
    xf = x.astype(F32)
    y = xf * lax.rsqrt(jnp.mean(xf * xf, axis=-1, keepdims=True) + EPS)
    return (y * g.astype(F32)).astype(x.dtype)


def _l2norm(x):
    return x * lax.rsqrt(jnp.sum(x * x, axis=-1, keepdims=True) + EPS)


def _short_conv(u, buf, w):
    T = u.shape[1]
    up = jnp.concatenate([buf.astype(u.dtype), u], axis=1)
    y = up[:, 0:T] * w[0]
    for j in range(1, A_CONV):
        y = y + up[:, j:j + T] * w[j]
    return jax.nn.silu(y), up[:, T:]


def _gated_delta(q, k, v, beta, g, S0):
    B, T, H, dk = k.shape
    dv = v.shape[-1]
    C = A_CHUNK
    n = -(-T // C)
    pad = n * C - T

    def chunks(a):
        a = jnp.pad(a, [(0, 0), (0, pad)] + [(0, 0)] * (a.ndim - 2))
        a = a.reshape((B, n, C) + a.shape[2:])
        return jnp.moveaxis(a, 3, 1)

    q, k, v, beta, g = (chunks(a) for a in (q, k, v, beta, g))
    gc = jnp.cumsum(g, axis=-1)
    causal = jnp.tril(jnp.ones((C, C), dtype=bool))
    strict = jnp.tril(jnp.ones((C, C), dtype=bool), -1)
    gdiff = gc[..., :, None] - gc[..., None, :]
    decay = jnp.where(causal, jnp.exp(jnp.where(causal, gdiff, 0.0)), 0.0)
    kb = k * beta[..., None]
    a_low = jnp.where(strict, jnp.einsum('bhnid,bhnjd->bhnij', kb, k) * decay, 0.0)
    rhs = jnp.concatenate([v * beta[..., None], kb * jnp.exp(gc)[..., None]], axis=-1)
    sol = lax.linalg.triangular_solve(a_low, rhs, left_side=True, lower=True, unit_diagonal=True)
    u, w = sol[..., :dv], sol[..., dv:]
    qk = jnp.einsum('bhnid,bhnjd->bhnij', q, k) * decay

    def step(S, xs):
        q_c, k_c, u_c, w_c, qk_c, gc_c = xs
        v_new = u_c - jnp.einsum('bhcd,bhde->bhce', w_c, S)
        o = (jnp.einsum('bhcd,bhde->bhce', q_c * jnp.exp(gc_c)[..., None], S)
             + jnp.einsum('bhij,bhje->bhie', qk_c, v_new))
        g_last = gc_c[..., -1:]
        S = (S * jnp.exp(g_last)[..., None]
             + jnp.einsum('bhcd,bhce->bhde', k_c * jnp.exp(g_last - gc_c)[..., None], v_new))
        return S, o

    xs = tuple(jnp.moveaxis(a, 2, 0) for a in (q, k, u, w, qk, gc))
    S, o = lax.scan(step, S0, xs)
    o = o.transpose(1, 0, 3, 2, 4).reshape(B, n * C, H, dv)[:, :T]
    return o, S


def _mixer_a_core(qkv, z, b, a, conv_buf, S0, conv_w, a_log, dt_bias, norm_g):
    B, T, _ = qkv.shape
    qkv, new_buf = _short_conv(qkv, conv_buf, conv_w)
    q, k, v = jnp.split(qkv, [A_KEY_DIM, 2 * A_KEY_DIM], axis=-1)
    rep = A_V_HEADS // A_K_HEADS
    q = jnp.repeat(_l2norm(q.reshape(B, T, A_K_HEADS, A_HEAD_DIM)) * (A_HEAD_DIM ** -0.5), rep, axis=2)
    k = jnp.repeat(_l2norm(k.reshape(B, T, A_K_HEADS, A_HEAD_DIM)), rep, axis=2)
    v = v.reshape(B, T, A_V_HEADS, A_HEAD_DIM)
    beta = jax.nn.sigmoid(b)
    g = -jnp.exp(a_log) * jax.nn.softplus(a + dt_bias)
    o, S = _gated_delta(q, k, v, beta, g, S0)
    o = _rmsnorm(o, norm_g) * jax.nn.silu(z.reshape(B, T, A_V_HEADS, A_HEAD_DIM))
    return o.reshape(B * T, A_VAL_DIM), new_buf, S


def _complex_affine_combine(e1, e2):
    a1r, a1i, b1r, b1i = e1
    a2r, a2i, b2r, b2i = e2
    return (a2r * a1r - a2i * a1i, a2r * a1i + a2i * a1r,
            a2r * b1r - a2i * b1i + b2r, a2r * b1i + a2i * b1r + b2i)


def _mixer_b_core(u, s_re, s_im, lam_re, lam_im, log_dt, b_re, b_im, c_re, c_im, d_skip):
    B, T, _ = u.shape
    ug = u.reshape(B, T, B_GROUPS, B_GROUP)
    dt = jnp.exp(log_dt)[:, None]
    mag = jnp.exp(lam_re * dt)
    ab_re, ab_im = mag * jnp.cos(lam_im * dt), mag * jnp.sin(lam_im * dt)
    den = lam_re * lam_re + lam_im * lam_im
    f_re = ((ab_re - 1.0) * lam_re + ab_im * lam_im) / den
    f_im = (ab_im * lam_re - (ab_re - 1.0) * lam_im) / den
    bb_re = f_re[..., None] * b_re - f_im[..., None] * b_im
    bb_im = f_re[..., None] * b_im + f_im[..., None] * b_re
    bu_re = jnp.einsum('btgc,gpc->btgp', ug, bb_re)
    bu_im = jnp.einsum('btgc,gpc->btgp', ug, bb_im)
    bu_re = bu_re.at[:, 0].add(ab_re * s_re - ab_im * s_im)
    bu_im = bu_im.at[:, 0].add(ab_re * s_im + ab_im * s_re)
    a_re = jnp.broadcast_to(ab_re, bu_re.shape)
    a_im = jnp.broadcast_to(ab_im, bu_im.shape)
    _, _, x_re, x_im = lax.associative_scan(_complex_affine_combine, (a_re, a_im, bu_re, bu_im), axis=1)
    y = jnp.einsum('btgp,gcp->btgc', x_re, c_re) - jnp.einsum('btgp,gcp->btgc', x_im, c_im)
    y = y.reshape(B, T, D_MODEL) + d_skip * u
    y = jax.nn.gelu(y)
    return y.reshape(B * T, D_MODEL), x_re[:, -1], x_im[:, -1]


def _attend(s, v, spec):
    m = jnp.max(s, axis=-1, keepdims=True)
    p = jnp.exp(s - m)
    l = jnp.sum(p, axis=-1)
    o = jnp.einsum(spec, p, v) / jnp.swapaxes(l, -1, -2)[..., None]
    lse = jnp.swapaxes(m[..., 0] + jnp.log(l), -1, -2)
    return o, lse


def _dilated_prompt(q, k, v, dil, nw):
    B, T, H, hd = q.shape
    L = T // dil
    nb = -(-L // nw)
    lp = nb * nw
    z = B * dil

    def by_residue(a):
        return a.reshape(B, L, dil, H, hd).transpose(0, 2, 1, 3, 4).reshape(z, L, H, hd)

    qs = jnp.pad(by_residue(q), ((0, 0), (0, lp - L), (0, 0), (0, 0))).reshape(z, nb, nw, H, hd)

    def key_blocks(a):
        ap = jnp.pad(by_residue(a), ((0, 0), (nw, lp - L), (0, 0), (0, 0)))
        return jnp.concatenate([ap[:, :lp].reshape(z, nb, nw, H, hd),
                                ap[:, nw:].reshape(z, nb, nw, H, hd)], axis=2)

    kb, vb = key_blocks(k), key_blocks(v)
    qi = jnp.arange(nw)[:, None]
    ks = jnp.arange(2 * nw)[None, :]
    dist = qi + nw - ks
    first = (jnp.arange(nb) * nw - nw)[:, None, None] + ks[None]
    mask = (dist >= 0) & (dist <= nw) & (first >= 0)
    s = jnp.einsum('znqhd,znkhd->znhqk', qs, kb) * (hd ** -0.5)
    s = jnp.where(mask[None, :, None], s, -jnp.inf)
    o, lse = _attend(s, vb, 'znhqk,znkhd->znqhd')
    o = o.reshape(z, lp, H, hd)[:, :L].reshape(B, dil, L, H, hd).transpose(0, 2, 1, 3, 4).reshape(B, T, H, hd)
    lse = lse.reshape(z, lp, H)[:, :L].reshape(B, dil, L, H).transpose(0, 2, 1, 3).reshape(B, T, H)
    return o, lse


def _dilated_sample(q, k, v, k_buf, v_buf, dil, nw):
    Tn = q.shape[1]
    lb = k_buf.shape[1]
    kc = jnp.concatenate([k_buf.astype(k.dtype), k], axis=1)
    vc = jnp.concatenate([v_buf.astype(v.dtype), v], axis=1)
    idx = lb + jnp.arange(Tn)[:, None] - dil * jnp.arange(nw + 1)[None, :]
    valid = idx >= 0
    idx = jnp.maximum(idx, 0)
    kg = jnp.take(kc, idx, axis=1)
    vg = jnp.take(vc, idx, axis=1)
    s = jnp.einsum('bqhd,bqkhd->bhqk', q, kg) * (q.shape[-1] ** -0.5)
    s = jnp.where(valid[None, None], s, -jnp.inf)
    o, lse = _attend(s, vg, 'bhqk,bqkhd->bqhd')
    return o, lse, kc[:, -lb:], vc[:, -lb:]


def _mixer_c_core(qkv, bufs, q_norm_g, k_norm_g):
    B, T, _ = qkv.shape
    qkv = qkv.reshape(B, T, C_GROUPS, 3, C_HEADS, C_HEAD_DIM)
    q = _rmsnorm(qkv[:, :, :, 0], q_norm_g[:, None, :])
    k = _rmsnorm(qkv[:, :, :, 1], k_norm_g[:, None, :])
    v = qkv[:, :, :, 2]
    outs, lses, new_bufs = [], [], []
    for gi in range(C_GROUPS):
        dil = C_DILATIONS[gi]
        nw = C_WINDOWS[gi] // dil
        qg, kg, vg = q[:, :, gi], k[:, :, gi], v[:, :, gi]
        if bufs is None:
            o, lse = _dilated_prompt(qg, kg, vg, dil, nw)
            lb = min(C_WINDOWS[gi], T)
            nk, nv = kg[:, T - lb:], vg[:, T - lb:]
        else:
            o, lse, nk, nv = _dilated_sample(qg, kg, vg, bufs[gi][0], bufs[gi][1], dil, nw)
        outs.append(o)
        lses.append(lse)
        new_bufs.append((nk, nv))
    wts = jax.nn.softmax(jnp.stack(lses), axis=0)
    o = jnp.sum(wts[..., None] * jnp.stack(outs), axis=0)
    return o.reshape(B * T, D_MODEL), new_bufs


def _route(logits, tm):
    M = logits.shape[0]
    top_v, top_i = lax.top_k(logits, TOP_K)
    gates = jax.nn.softmax(top_v, axis=-1)
    e_flat = top_i.reshape(-1)
    onehot = (e_flat[:, None] == jnp.arange(N_EXPERTS)[None, :]).astype(jnp.int32)
    counts = jnp.sum(onehot, axis=0)
    rank = jnp.sum((jnp.cumsum(onehot, axis=0) - onehot) * onehot, axis=1)
    padded = ((counts + tm - 1) // tm) * tm
    ends = jnp.cumsum(padded)
    pos = (ends - padded)[e_flat] + rank
    nt = (TOP_K * M + N_EXPERTS * (tm - 1)) // tm
    src = jnp.zeros((nt * tm,), jnp.int32).at[pos].set(jnp.arange(TOP_K * M, dtype=jnp.int32) // TOP_K)
    starts = jnp.arange(nt, dtype=jnp.int32) * tm
    tile_expert = jnp.sum((starts[:, None] >= ends[None, :]).astype(jnp.int32), axis=1)
    tile_valid = (starts < ends[-1]).astype(jnp.int32)
    last_expert = tile_expert[ends[-1] // tm - 1]
    tile_expert = jnp.where(tile_valid > 0, tile_expert, last_expert).astype(jnp.int32)
    return gates, pos.reshape(M, TOP_K), src, tile_expert, tile_valid


def _trunk(x, mod, st, p, *, tm, tm_moe):
    B, T, _ = x.shape
    M = B * T
    per_row = tm > T

    def vec(v):
        if per_row:
            return jnp.repeat(v, T, axis=0).reshape(1, M, D_MODEL)
        return v.reshape(B, 1, D_MODEL)

    tm_big_k = min(tm, 512)
    x = x.reshape(M, D_MODEL)
    a_conv, a_rec, b_re, b_im = [], [], [], []
    c_k = [[] for _ in range(C_GROUPS)]
    c_v = [[] for _ in range(C_GROUPS)]
    for i in range(DEPTH):
        sh1, sc1, gt1, sh2, sc2, gt2 = (vec(m) for m in jnp.split(mod[i], N_MOD, axis=-1))
        kind, slot = i % N_MIXERS, i // N_MIXERS
        if kind == 0:
            h = _modulate(x, p['g_mix'][i], sh1, sc1, tm=tm)[0]
            main = _linear(h, p['a_w_in'], (slot,), n_out=A_CONV_DIM + A_VAL_DIM, tm=tm, tn=512)
            small = _linear(h, p['a_w_in'], (slot,), n_out=LANE, tm=tm, tn=LANE,
                            col_off=(A_CONV_DIM + A_VAL_DIM) // LANE)
            qkv = main[:, :A_CONV_DIM].reshape(B, T, A_CONV_DIM)
            z = main[:, A_CONV_DIM:].reshape(B, T, A_VAL_DIM)
            bb = small[:, :A_V_HEADS].reshape(B, T, A_V_HEADS)
            aa = small[:, A_V_HEADS:2 * A_V_HEADS].reshape(B, T, A_V_HEADS)
            if st is None:
                buf0 = jnp.zeros((B, A_CONV - 1, A_CONV_DIM), F32)
                s0 = jnp.zeros((B, A_V_HEADS, A_HEAD_DIM, A_HEAD_DIM), F32)
            else:
                buf0, s0 = st['a_conv'][slot], st['a_rec'][slot]
            o, nbuf, ns = _mixer_a_core(qkv, z, bb, aa, buf0, s0, p['a_conv_w'][slot], p['a_log'][slot],
                                        p['a_dt_bias'][slot], p['a_norm_g'][slot])
            a_conv.append(nbuf)
            a_rec.append(ns)
            x = _linear(o.astype(BF16), p['a_w_out'], (slot,), n_out=D_MODEL, tm=tm_big_k, tn=512,
                        mode="resid", res=x, gate=gt1)
        elif kind == 1:
            h, hf = _modulate(x, p['g_mix'][i], sh1, sc1, tm=tm, extra="f32")
            if st is None:
                r0 = jnp.zeros((B, B_GROUPS, B_STATE), F32)
                m0 = jnp.zeros((B, B_GROUPS, B_STATE), F32)
            else:
                r0, m0 = st['b_re'][slot], st['b_im'][slot]
            y, nr, ni = _mixer_b_core(hf.reshape(B, T, D_MODEL), r0, m0, p['b_lambda_re'][slot],
                                      p['b_lambda_im'][slot], p['b_log_dt'][slot], p['b_B_re'][slot],
                                      p['b_B_im'][slot], p['b_C_re'][slot], p['b_C_im'][slot], p['b_D'][slot])
            b_re.append(nr)
            b_im.append(ni)
            x = _linear(y.astype(BF16), p['b_w_glu'], (slot,), n_out=D_MODEL, tm=tm, tn=512,
                        mode="glu_resid", col_off2=D_MODEL // 512, res=x, gate=gt1)
        else:
            h = _modulate(x, p['g_mix'][i], sh1, sc1, tm=tm)[0]
            qkv = _linear(h, p['c_w_qkv'], (slot,), n_out=C_GROUPS * 3 * D_MODEL, tm=tm, tn=512)
            bufs = None if st is None else [(st['c_k'][g][slot], st['c_v'][g][slot]) for g in range(C_GROUPS)]
            o, nbs = _mixer_c_core(qkv.reshape(B, T, -1), bufs, p['c_q_norm_g'][slot], p['c_k_norm_g'][slot])
            for g in range(C_GROUPS):
                c_k[g].append(nbs[g][0])
                c_v[g].append(nbs[g][1])
            x = _linear(o.astype(BF16), p['c_w_out'], (slot,), n_out=D_MODEL, tm=tm, tn=512,
                        mode="resid", res=x, gate=gt1)
        if i % 2 == 0:
            h = _modulate(x, p['g_ffn'][i], sh2, sc2, tm=tm)[0]
            act = _linear(h, p['f_w_gate_up'], (i // 2,), n_out=F_DENSE, tm=tm, tn=512, mode="swiglu",
                          col_off2=F_DENSE // 512, out_dtype=BF16)
            x = _linear(act, p['f_w_down'], (i // 2,), n_out=D_MODEL, tm=tm_big_k, tn=512,
                        mode="resid", res=x, gate=gt2)
        else:
            router = jnp.pad(p['m_router'][i // 2], ((0, 0), (0, LANE - N_EXPERTS)))
            h, logits = _modulate(x, p['g_ffn'][i], sh2, sc2, tm=tm, extra="router", router=router)
            gates, pos, src, tile_expert, tile_valid = _route(logits[:, :N_EXPERTS], tm_moe)
            xs = jnp.take(h, src, axis=0)
            act = _moe_linear(xs, p['m_w_gate_up'], i // 2, tile_expert, tile_valid,
                              n_out=F_EXPERT, tm=tm_moe, tn=512, up=True)
            ys = _moe_linear(act, p['m_w_down'], i // 2, tile_expert, tile_valid,
                             n_out=D_MODEL, tm=tm_moe, tn=512, up=False)
            y = (gates[:, 0:1] * jnp.take(ys, pos[:, 0], axis=0)
                 + gates[:, 1:2] * jnp.take(ys, pos[:, 1], axis=0))
            gt2_rows = gt2.reshape(M, D_MODEL) if per_row else jnp.repeat(gt2.reshape(B, D_MODEL), T, axis=0)
            x = x + gt2_rows * y
    stk = lambda lst: jnp.stack(lst, axis=0)
    states = (stk(a_conv), stk(a_rec), stk(b_re), stk(b_im), stk(c_k[0]), stk(c_v[0]),
              stk(c_k[1]), stk(c_v[1]), stk(c_k[2]), stk(c_v[2]))
    return x.reshape(B, T, D_MODEL), states


def kernel(x_prompt, x_sample, c_prompt, c_sample, state_a_conv, state_a_rec, state_b_re, state_b_im, cache_c_k0, cache_c_v0, cache_c_k1, cache_c_v1, cache_c_k2, cache_c_v2, g_mix, g_ffn, w_mod, b_mod, a_w_in, a_conv_w, a_log, a_dt_bias, a_norm_g, a_w_out, b_lambda_re, b_lambda_im, b_log_dt, b_B_re, b_B_im, b_C_re, b_C_im, b_D, b_w_glu, c_w_qkv, c_q_norm_g, c_k_norm_g, c_w_out, f_w_gate_up, f_w_down, m_router, m_w_gate_up, m_w_down):
    p = dict(g_mix=g_mix, g_ffn=g_ffn,
             a_w_in=a_w_in, a_conv_w=a_conv_w, a_log=a_log, a_dt_bias=a_dt_bias, a_norm_g=a_norm_g, a_w_out=a_w_out,
             b_lambda_re=b_lambda_re, b_lambda_im=b_lambda_im, b_log_dt=b_log_dt, b_B_re=b_B_re, b_B_im=b_B_im,
             b_C_re=b_C_re, b_C_im=b_C_im, b_D=b_D, b_w_glu=b_w_glu,
             c_w_qkv=c_w_qkv, c_q_norm_g=c_q_norm_g, c_k_norm_g=c_k_norm_g, c_w_out=c_w_out,
             f_w_gate_up=f_w_gate_up, f_w_down=f_w_down,
             m_router=m_router, m_w_gate_up=m_w_gate_up, m_w_down=m_w_down)
    st = dict(a_conv=state_a_conv, a_rec=state_a_rec, b_re=state_b_re, b_im=state_b_im,
              c_k=(cache_c_k0, cache_c_k1, cache_c_k2), c_v=(cache_c_v0, cache_c_v1, cache_c_v2))
    nb_p, nb_s = c_prompt.shape[0], c_sample.shape[0]
    c_all = jnp.concatenate([c_prompt, c_sample, jnp.zeros((16 - nb_p - nb_s, D_MODEL), F32)], axis=0)
    mod = _adaln(c_all, w_mod, b_mod)
    y_prompt, ps = _trunk(x_prompt, mod[:, :nb_p], None, p, tm=1024, tm_moe=256)
    y_sample, ss = _trunk(x_sample, mod[:, nb_p:nb_p + nb_s], st, p, tm=64, tm_moe=16)
    return (y_prompt, y_sample) + ps + ss
```

```python
import functools

import jax
import jax.numpy as jnp
from jax import lax
from jax.experimental import pallas as pl
from jax.experimental.pallas import tpu as pltpu

D_MODEL = 2048
DEPTH = 4
N_MIXERS = 3
N_MOD = 6
EPS = 1e-6

A_K_HEADS = 16
A_V_HEADS = 32
A_HEAD_DIM = 128
A_KEY_DIM = A_K_HEADS * A_HEAD_DIM
A_VAL_DIM = A_V_HEADS * A_HEAD_DIM
A_CONV_DIM = 2 * A_KEY_DIM + A_VAL_DIM
A_CONV = 4
A_CHUNK = 64

B_GROUP = 16
B_GROUPS = D_MODEL // B_GROUP
B_STATE = 64

C_HEADS = 16
C_HEAD_DIM = D_MODEL // C_HEADS
C_WINDOWS = (128, 512, 2048)
C_DILATIONS = (1, 4, 16)
C_GROUPS = 3

F_DENSE = 5632
N_EXPERTS = 8
TOP_K = 2
F_EXPERT = 7168

LANE = 128
VMEM_LIMIT = 56 * 1024 * 1024
BF16 = jnp.bfloat16
F32 = jnp.float32


def _params(n_grid):
    return pltpu.CompilerParams(dimension_semantics=("arbitrary",) * n_grid,
                                vmem_limit_bytes=VMEM_LIMIT)


def _cast_weights_on_first_row_tile(w_refs, wb_refs):
    @pl.when(pl.program_id(1) == 0)
    def _():
        for w_ref, wb_ref in zip(w_refs, wb_refs):
            wb_ref[...] = w_ref[...].astype(BF16)


def _dot(x, wb_ref):
    return jnp.dot(x, wb_ref[...], preferred_element_type=F32)


def _k_plain(x_ref, w_ref, o_ref, wb_ref):
    _cast_weights_on_first_row_tile((w_ref,), (wb_ref,))
    o_ref[...] = _dot(x_ref[...], wb_ref).astype(o_ref.dtype)


def _k_resid(x_ref, w_ref, r_ref, g_ref, o_ref, wb_ref):
    _cast_weights_on_first_row_tile((w_ref,), (wb_ref,))
    o_ref[...] = r_ref[...] + g_ref[...] * _dot(x_ref[...], wb_ref)


def _k_swiglu(x_ref, wg_ref, wu_ref, o_ref, wgb_ref, wub_ref):
    _cast_weights_on_first_row_tile((wg_ref, wu_ref), (wgb_ref, wub_ref))
    x = x_ref[...]
    g = _dot(x, wgb_ref)
    u = _dot(x, wub_ref)
    o_ref[...] = (g * jax.nn.sigmoid(g) * u).astype(o_ref.dtype)


def _k_glu_resid(x_ref, wa_ref, wb_ref, r_ref, g_ref, o_ref, wab_ref, wbb_ref):
    _cast_weights_on_first_row_tile((wa_ref, wb_ref), (wab_ref, wbb_ref))
    x = x_ref[...]
    ya = _dot(x, wab_ref)
    yb = _dot(x, wbb_ref)
    o_ref[...] = r_ref[...] + g_ref[...] * (ya * jax.nn.sigmoid(yb))


def _linear(x, w, widx, *, n_out, tm, tn, mode="plain", col_off=0, col_off2=0,
            res=None, gate=None, out_dtype=F32):
    M, K = x.shape
    assert M % tm == 0 and n_out % tn == 0
    ni, nj = M // tm, n_out // tn
    lead = (None,) * len(widx)

    def wspec(off):
        return pl.BlockSpec(lead + (K, tn), lambda j, i: tuple(widx) + (0, j + off))

    xspec = pl.BlockSpec((tm, K), lambda j, i: (i, 0))
    ospec = pl.BlockSpec((tm, tn), lambda j, i: (i, j))
    two = mode in ("swiglu", "glu_resid")
    in_specs = [xspec, wspec(col_off)] + ([wspec(col_off2)] if two else [])
    args = [x, w] + ([w] if two else [])
    if mode in ("resid", "glu_resid"):
        tiles_per_gate = ni // gate.shape[0]
        in_specs += [ospec,
                     pl.BlockSpec((None, gate.shape[1], tn),
                                  lambda j, i: (i // tiles_per_gate, 0, j))]
        args += [res, gate]
    body = {"plain": _k_plain, "resid": _k_resid, "swiglu": _k_swiglu,
            "glu_resid": _k_glu_resid}[mode]
    return pl.pallas_call(
        body,
        grid=(nj, ni),
        in_specs=in_specs,
        out_specs=ospec,
        out_shape=jax.ShapeDtypeStruct((M, n_out), out_dtype),
        scratch_shapes=[pltpu.VMEM((K, tn), BF16)] * (2 if two else 1),
        compiler_params=_params(2),
        name="linear_" + mode,
    )(*args)


def _cast_weights_on_expert_change(te_ref, w_refs, wb_refs):
    t = pl.program_id(1)
    prev = te_ref[jnp.maximum(t - 1, 0)]

    @pl.when((t == 0) | (te_ref[t] != prev))
    def _():
        for w_ref, wb_ref in zip(w_refs, wb_refs):
            wb_ref[...] = w_ref[...].astype(BF16)


def _k_moe_up(te_ref, tv_ref, x_ref, wg_ref, wu_ref, o_ref, wgb_ref, wub_ref):
    _cast_weights_on_expert_change(te_ref, (wg_ref, wu_ref), (wgb_ref, wub_ref))
    valid = tv_ref[pl.program_id(1)] > 0

    @pl.when(valid)
    def _():
        x = x_ref[...]
        g = _dot(x, wgb_ref)
        u = _dot(x, wub_ref)
        o_ref[...] = (g * jax.nn.sigmoid(g) * u).astype(o_ref.dtype)

    @pl.when(jnp.logical_not(valid))
    def _():
        o_ref[...] = jnp.zeros_like(o_ref)


def _k_moe_down(te_ref, tv_ref, x_ref, w_ref, o_ref, wb_ref):
    _cast_weights_on_expert_change(te_ref, (w_ref,), (wb_ref,))
    valid = tv_ref[pl.program_id(1)] > 0

    @pl.when(valid)
    def _():
        o_ref[...] = _dot(x_ref[...], wb_ref)

    @pl.when(jnp.logical_not(valid))
    def _():
        o_ref[...] = jnp.zeros_like(o_ref)


def _moe_linear(x, w, layer, tile_expert, tile_valid, *, n_out, tm, tn, up):
    Mp, K = x.shape
    nt, nj = Mp // tm, n_out // tn

    def wspec(off):
        return pl.BlockSpec((None, None, K, tn),
                            lambda j, t, te, tv: (layer, te[t], 0, j + off))

    xspec = pl.BlockSpec((tm, K), lambda j, t, te, tv: (t, 0))
    ospec = pl.BlockSpec((tm, tn), lambda j, t, te, tv: (t, j))
    in_specs = [xspec, wspec(0)] + ([wspec(nj)] if up else [])
    args = [x, w] + ([w] if up else [])
    return pl.pallas_call(
        _k_moe_up if up else _k_moe_down,
        grid_spec=pltpu.PrefetchScalarGridSpec(
            num_scalar_prefetch=2,
            grid=(nj, nt),
            in_specs=in_specs,
            out_specs=ospec,
            scratch_shapes=[pltpu.VMEM((K, tn), BF16)] * (2 if up else 1)),
        out_shape=jax.ShapeDtypeStruct((Mp, n_out), BF16 if up else F32),
        compiler_params=_params(2),
        name="moe_up" if up else "moe_down",
    )(tile_expert, tile_valid, *args)


def _k_adaln(c_ref, w_ref, b_ref, o_ref):
    c = c_ref[...]
    a = (c * jax.nn.sigmoid(c)).astype(BF16)
    o_ref[...] = jnp.dot(a, w_ref[...].astype(BF16), preferred_element_type=F32) + b_ref[...]


def _adaln(c_all, w_mod, b_mod):
    R = c_all.shape[0]
    tn = 1024
    n = N_MOD * D_MODEL
    return pl.pallas_call(
        _k_adaln,
        grid=(DEPTH, n // tn),
        in_specs=[pl.BlockSpec((R, D_MODEL), lambda l, j: (0, 0)),
                  pl.BlockSpec((None, D_MODEL, tn), lambda l, j: (l, 0, j)),
                  pl.BlockSpec((None, 1, tn), lambda l, j: (l, 0, j))],
        out_specs=pl.BlockSpec((None, R, tn), lambda l, j: (l, 0, j)),
        out_shape=jax.ShapeDtypeStruct((DEPTH, R, n), F32),
        compiler_params=_params(2),
        name="adaln",
    )(c_all, w_mod, b_mod.reshape(DEPTH, 1, n))


def _modulated(x_ref, g_ref, sh_ref, sc_ref):
    x = x_ref[...]
    y = x * lax.rsqrt(jnp.mean(x * x, axis=-1, keepdims=True) + EPS)
    return (y * g_ref[...]) * (1.0 + sc_ref[...]) + sh_ref[...]


def _k_modulate(x_ref, g_ref, sh_ref, sc_ref, hb_ref):
    hb_ref[...] = _modulated(x_ref, g_ref, sh_ref, sc_ref).astype(BF16)


def _k_modulate_f32(x_ref, g_ref, sh_ref, sc_ref, hb_ref, hf_ref):
    h = _modulated(x_ref, g_ref, sh_ref, sc_ref)
    hb_ref[...] = h.astype(BF16)
    hf_ref[...] = h


def _k_modulate_router(x_ref, g_ref, sh_ref, sc_ref, r_ref, hb_ref, lg_ref):
    h = _modulated(x_ref, g_ref, sh_ref, sc_ref)
    hb = h.astype(BF16)
    hb_ref[...] = hb
    lg_ref[...] = jnp.dot(hb, r_ref[...].astype(BF16), preferred_element_type=F32)


def _modulate(x, g, shift, scale, *, tm, extra=None, router=None):
    M = x.shape[0]
    ni = M // tm
    tiles_per_vec = ni // shift.shape[0]
    xspec = pl.BlockSpec((tm, D_MODEL), lambda i: (i, 0))
    vspec = pl.BlockSpec((None, shift.shape[1], D_MODEL), lambda i: (i // tiles_per_vec, 0, 0))
    in_specs = [xspec, pl.BlockSpec((1, D_MODEL), lambda i: (0, 0)), vspec, vspec]
    args = [x, g.reshape(1, D_MODEL), shift, scale]
    out_specs = [xspec]
    out_shape = [jax.ShapeDtypeStruct((M, D_MODEL), BF16)]
    body = _k_modulate
    if extra == "f32":
        body = _k_modulate_f32
        out_specs.append(xspec)
        out_shape.append(jax.ShapeDtypeStruct((M, D_MODEL), F32))
    elif extra == "router":
        body = _k_modulate_router
        in_specs.append(pl.BlockSpec((D_MODEL, LANE), lambda i: (0, 0)))
        args.append(router)
        out_specs.append(pl.BlockSpec((tm, LANE), lambda i: (i, 0)))
        out_shape.append(jax.ShapeDtypeStruct((M, LANE), F32))
    return pl.pallas_call(
        body, grid=(ni,), in_specs=in_specs, out_specs=out_specs, out_shape=out_shape,
        compiler_params=_params(1), name="modulate",
    )(*args)


S5_SUPER = 16
S5_ROWS = 8
HI = lax.Precision.HIGHEST


def _k_s5(u_ref, bb_ref, cre_ref, cim_ref, are_ref, aim_ref, d_ref, s0re_ref, s0im_ref,
          y_ref, fre_ref, fim_ref, xre_ref, xim_ref, car_re, car_im, *, tc):
    @pl.when(pl.program_id(0) == 0)
    def _():
        car_re[...] = s0re_ref[...]
        car_im[...] = s0im_ref[...]

    ub = u_ref[...].astype(BF16)
    for s in range(S5_SUPER):
        bu = jnp.dot(ub[:, s * 128:(s + 1) * 128], bb_ref[s], preferred_element_type=F32)
        xre_ref[:, s * 512:(s + 1) * 512] = bu[:, :512]
        xim_ref[:, s * 512:(s + 1) * 512] = bu[:, 512:]

    for s in range(S5_SUPER):
        cols = slice(s * 512, (s + 1) * 512)
        a_re = jnp.broadcast_to(are_ref[:, cols], (S5_ROWS, 512))
        a_im = jnp.broadcast_to(aim_ref[:, cols], (S5_ROWS, 512))

        def body(t, carry, cols=cols, a_re=a_re, a_im=a_im):
            xr, xi = carry
            r0 = pl.multiple_of(t * S5_ROWS, S5_ROWS)
            nr = (a_re * xr - a_im * xi) + xre_ref[pl.ds(r0, S5_ROWS), cols]
            ni = (a_re * xi + a_im * xr) + xim_ref[pl.ds(r0, S5_ROWS), cols]
            xre_ref[pl.ds(r0, S5_ROWS), cols] = nr
            xim_ref[pl.ds(r0, S5_ROWS), cols] = ni
            return nr, ni

        xr, xi = lax.fori_loop(0, tc, body, (car_re[:, cols], car_im[:, cols]))
        car_re[:, cols] = xr
        car_im[:, cols] = xi

    for s in range(S5_SUPER):
        cols = slice(s * 512, (s + 1) * 512)
        lanes = slice(s * 128, (s + 1) * 128)
        y = (jnp.dot(xre_ref[:, cols].astype(BF16), cre_ref[s], preferred_element_type=F32)
             - jnp.dot(xim_ref[:, cols].astype(BF16), cim_ref[s], preferred_element_type=F32))
        y = y + d_ref[:, lanes] * u_ref[:, lanes]
        y_ref[:, lanes] = jax.nn.gelu(y).astype(BF16)
    fre_ref[...] = car_re[...]
    fim_ref[...] = car_im[...]


def _s5_core(u, s_re, s_im, lam_re, lam_im, log_dt, b_re, b_im, c_re, c_im, d_skip, *, tc):
    B, T, D = u.shape
    G, P = lam_re.shape
    Cg = D // G
    assert G == S5_SUPER * 8 and 8 * Cg == LANE and B <= S5_ROWS and T % tc == 0
    dt = jnp.exp(log_dt)[:, None]
    mag = jnp.exp(lam_re * dt)
    ab_re, ab_im = mag * jnp.cos(lam_im * dt), mag * jnp.sin(lam_im * dt)
    den = lam_re * lam_re + lam_im * lam_im
    f_re = ((ab_re - 1.0) * lam_re + ab_im * lam_im) / den
    f_im = (ab_im * lam_re - (ab_re - 1.0) * lam_im) / den
    bb_re = f_re[..., None] * b_re - f_im[..., None] * b_im
    bb_im = f_re[..., None] * b_im + f_im[..., None] * b_re
    eye = jnp.eye(8, dtype=F32)

    def blockdiag_in(bb):
        t = bb.reshape(S5_SUPER, 8, P, Cg)
        return jnp.einsum('sgpc,gh->sgchp', t, eye).reshape(S5_SUPER, 8 * Cg, 8 * P)

    def blockdiag_out(c):
        t = c.reshape(S5_SUPER, 8, Cg, P)
        return jnp.einsum('sgcp,gh->sgphc', t, eye).reshape(S5_SUPER, 8 * P, 8 * Cg)

    bb = jnp.concatenate([blockdiag_in(bb_re), blockdiag_in(bb_im)], axis=-1).astype(BF16)
    cre = blockdiag_out(c_re).astype(BF16)
    cim = blockdiag_out(c_im).astype(BF16)
    pad = S5_ROWS - B
    u_tb = jnp.pad(jnp.transpose(u, (1, 0, 2)), ((0, 0), (0, pad), (0, 0))).reshape(T * S5_ROWS, D)
    s0re = jnp.pad(s_re.reshape(B, G * P), ((0, pad), (0, 0)))
    s0im = jnp.pad(s_im.reshape(B, G * P), ((0, pad), (0, 0)))
    R = tc * S5_ROWS
    NS = G * P
    full = lambda shp: pl.BlockSpec(shp, lambda i: (0,) * len(shp))
    y, fre, fim = pl.pallas_call(
        functools.partial(_k_s5, tc=tc),
        grid=(T // tc,),
        in_specs=[pl.BlockSpec((R, D), lambda i: (i, 0)),
                  full((S5_SUPER, 8 * Cg, 16 * P)), full((S5_SUPER, 8 * P, 8 * Cg)),
                  full((S5_SUPER, 8 * P, 8 * Cg)),
                  full((1, NS)), full((1, NS)), full((1, D)), full((S5_ROWS, NS)), full((S5_ROWS, NS))],
        out_specs=[pl.BlockSpec((R, D), lambda i: (i, 0)), full((S5_ROWS, NS)), full((S5_ROWS, NS))],
        out_shape=[jax.ShapeDtypeStruct((T * S5_ROWS, D), BF16),
                   jax.ShapeDtypeStruct((S5_ROWS, NS), F32), jax.ShapeDtypeStruct((S5_ROWS, NS), F32)],
        scratch_shapes=[pltpu.VMEM((R, NS), F32), pltpu.VMEM((R, NS), F32),
                        pltpu.VMEM((S5_ROWS, NS), F32), pltpu.VMEM((S5_ROWS, NS), F32)],
        compiler_params=_params(1),
        name="s5_core",
    )(u_tb, bb, cre, cim, ab_re.reshape(1, NS), ab_im.reshape(1, NS), d_skip.reshape(1, D), s0re, s0im)
    y = jnp.transpose(y.reshape(T, S5_ROWS, D)[:, :B], (1, 0, 2)).reshape(B * T, D)
    return y, fre[:B].reshape(B, G, P), fim[:B].reshape(B, G, P)


def _k_band_attn(q_ref, kc_ref, kp_ref, vc_ref, vp_ref, qg_ref, kg_ref, o_ref, kn_ref, lse_ref,
                 *, heads, hd, nw):
    bi = pl.program_id(1)
    qi = lax.broadcasted_iota(jnp.int32, (nw, nw), 0)
    kj = lax.broadcasted_iota(jnp.int32, (nw, nw), 1)
    cur_ok = kj <= qi
    prev_ok = jnp.logical_and(kj >= qi, bi > 0)
    lane = lax.broadcasted_iota(jnp.int32, (nw, LANE), 1)
    scale = hd ** -0.5
    qg = qg_ref[...]
    kg = kg_ref[...]
    lse_tile = jnp.zeros((nw, LANE), F32)

    def rms(x, g):
        return (x * lax.rsqrt(jnp.mean(x * x, axis=-1, keepdims=True) + EPS)) * g

    nt = (((1,), (1,)), ((), ()))
    for h in range(heads):
        sl = slice(h * hd, (h + 1) * hd)
        q = rms(q_ref[:, sl], qg).astype(BF16)
        kc = rms(kc_ref[:, sl], kg)
        kn_ref[:, sl] = kc
        kp = rms(kp_ref[:, sl], kg).astype(BF16)
        s_c = lax.dot_general(q, kc.astype(BF16), nt, preferred_element_type=F32) * scale
        s_p = lax.dot_general(q, kp, nt, preferred_element_type=F32) * scale
        s_c = jnp.where(cur_ok, s_c, -jnp.inf)
        s_p = jnp.where(prev_ok, s_p, -jnp.inf)
        m = jnp.maximum(jnp.max(s_c, axis=-1, keepdims=True), jnp.max(s_p, axis=-1, keepdims=True))
        p_c = jnp.exp(s_c - m)
        p_p = jnp.exp(s_p - m)
        l = jnp.sum(p_c, axis=-1, keepdims=True) + jnp.sum(p_p, axis=-1, keepdims=True)
        o = (jnp.dot(p_p.astype(BF16), vp_ref[:, sl].astype(BF16), preferred_element_type=F32)
             + jnp.dot(p_c.astype(BF16), vc_ref[:, sl].astype(BF16), preferred_element_type=F32))
        o_ref[:, sl] = o / l
        lse_tile = jnp.where(lane == h, m + jnp.log(l), lse_tile)
    lse_ref[...] = lse_tile


def _band_attn(qkv, q_g, k_g, *, z, L, heads, hd, nw):
    W = heads * hd
    nb = L // nw
    x = qkv.reshape(z, L, 3 * W)

    def blk(col, prev):
        if prev:
            return pl.BlockSpec((None, nw, W), lambda zi, bi: (zi, jnp.maximum(bi - 1, 0), col))
        return pl.BlockSpec((None, nw, W), lambda zi, bi: (zi, bi, col))

    gspec = pl.BlockSpec((1, hd), lambda zi, bi: (0, 0))
    ospec = pl.BlockSpec((None, nw, W), lambda zi, bi: (zi, bi, 0))
    lspec = pl.BlockSpec((None, nw, LANE), lambda zi, bi: (zi, bi, 0))
    o, kn, lse = pl.pallas_call(
        functools.partial(_k_band_attn, heads=heads, hd=hd, nw=nw),
        grid=(z, nb),
        in_specs=[blk(0, False), blk(1, False), blk(1, True), blk(2, False), blk(2, True), gspec, gspec],
        out_specs=[ospec, ospec, lspec],
        out_shape=[jax.ShapeDtypeStruct((z, L, W), F32), jax.ShapeDtypeStruct((z, L, W), F32),
                   jax.ShapeDtypeStruct((z, L, LANE), F32)],
        compiler_params=_params(2),
        name="band_attn",
    )(x, x, x, x, x, q_g.reshape(1, hd), k_g.reshape(1, hd))
    return o.reshape(z * L, W), kn.reshape(z * L, W), lse.reshape(z * L, LANE)


SOLVE_PACK = 4


def _k_unit_lower_solve(a_ref, r_ref, o_ref, ablk_ref, *, c, groups):
    @pl.when(pl.program_id(0) == 0)
    def _():
        ablk_ref[...] = jnp.zeros_like(ablk_ref)

    n = SOLVE_PACK * c
    w = r_ref.shape[-1]
    for g in range(groups):
        for i in range(SOLVE_PACK):
            ablk_ref[g, i * c:(i + 1) * c, i * c:(i + 1) * c] = a_ref[g * SOLVE_PACK + i]
    dot = lambda x, y: jnp.dot(x, y, preferred_element_type=F32, precision=HI)
    for g in range(groups):
        a = ablk_ref[g]
        r = r_ref[g * SOLVE_PACK:(g + 1) * SOLVE_PACK].reshape(n, w)
        x = r - dot(a, r)
        p = a
        k = 1
        while 2 * k < c:
            p = dot(p, p)
            x = x + dot(p, x)
            k *= 2
        o_ref[g * SOLVE_PACK:(g + 1) * SOLVE_PACK] = x.reshape(SOLVE_PACK, c, w)


def _unit_lower_solve(a, rhs, *, groups=2):
    N, c, _ = a.shape
    w = rhs.shape[-1]
    per = SOLVE_PACK * groups
    assert N % per == 0
    return pl.pallas_call(
        functools.partial(_k_unit_lower_solve, c=c, groups=groups),
        grid=(N // per,),
        in_specs=[pl.BlockSpec((per, c, c), lambda i: (i, 0, 0)),
                  pl.BlockSpec((per, c, w), lambda i: (i, 0, 0))],
        out_specs=pl.BlockSpec((per, c, w), lambda i: (i, 0, 0)),
        out_shape=jax.ShapeDtypeStruct((N, c, w), F32),
        scratch_shapes=[pltpu.VMEM((groups, SOLVE_PACK * c, SOLVE_PACK * c), F32)],
        compiler_params=_params(1),
        name="unit_lower_solve",
    )(a, rhs)


def _rmsnorm(x, g):
    xf = x.astype(F32)
    y = xf * lax.rsqrt(jnp.mean(xf * xf, axis=-1, keepdims=True) + EPS)
    return (y * g.astype(F32)).astype(x.dtype)


def _l2norm(x):
    return x * lax.rsqrt(jnp.sum(x * x, axis=-1, keepdims=True) + EPS)


def _short_conv(u, buf, w):
    T = u.shape[1]
    up = jnp.concatenate([buf.astype(u.dtype), u], axis=1)
    y = up[:, 0:T] * w[0]
    for j in range(1, A_CONV):
        y = y + up[:, j:j + T] * w[j]
    return jax.nn.silu(y), up[:, T:]


def _gated_delta(q, k, v, beta, g, S0):
    B, T, H, dk = k.shape
    dv = v.shape[-1]
    C = A_CHUNK
    n = -(-T // C)
    pad = n * C - T

    def chunks(a):
        a = jnp.pad(a, [(0, 0), (0, pad)] + [(0, 0)] * (a.ndim - 2))
        a = a.reshape((B, n, C) + a.shape[2:])
        return jnp.moveaxis(a, 3, 1)

    q, k, v, beta, g = (chunks(a) for a in (q, k, v, beta, g))
    gc = jnp.cumsum(g, axis=-1)
    causal = jnp.tril(jnp.ones((C, C), dtype=bool))
    strict = jnp.tril(jnp.ones((C, C), dtype=bool), -1)
    gdiff = gc[..., :, None] - gc[..., None, :]
    decay = jnp.where(causal, jnp.exp(jnp.where(causal, gdiff, 0.0)), 0.0)
    kb = k * beta[..., None]
    a_low = jnp.where(strict, jnp.einsum('bhnid,bhnjd->bhnij', kb, k) * decay, 0.0)
    rhs = jnp.concatenate([v * beta[..., None], kb * jnp.exp(gc)[..., None]], axis=-1)
    sol = _unit_lower_solve(a_low.reshape(B * H * n, C, C), rhs.reshape(B * H * n, C, dv + dk))
    sol = sol.reshape(B, H, n, C, dv + dk)
    u, w = sol[..., :dv], sol[..., dv:]
    qk = jnp.einsum('bhnid,bhnjd->bhnij', q, k) * decay

    def step(S, xs):
        q_c, k_c, u_c, w_c, qk_c, gc_c = xs
        v_new = u_c - jnp.einsum('bhcd,bhde->bhce', w_c, S)
        o = (jnp.einsum('bhcd,bhde->bhce', q_c * jnp.exp(gc_c)[..., None], S)
             + jnp.einsum('bhij,bhje->bhie', qk_c, v_new))
        g_last = gc_c[..., -1:]
        S = (S * jnp.exp(g_last)[..., None]
             + jnp.einsum('bhcd,bhce->bhde', k_c * jnp.exp(g_last - gc_c)[..., None], v_new))
        return S, o

    xs = tuple(jnp.moveaxis(a, 2, 0) for a in (q, k, u, w, qk, gc))
    S, o = lax.scan(step, S0, xs)
    o = o.transpose(1, 0, 3, 2, 4).reshape(B, n * C, H, dv)[:, :T]
    return o, S


def _mixer_a_core(qkv, z, b, a, conv_buf, S0, conv_w, a_log, dt_bias, norm_g):
    B, T, _ = qkv.shape
    qkv, new_buf = _short_conv(qkv, conv_buf, conv_w)
    q, k, v = jnp.split(qkv, [A_KEY_DIM, 2 * A_KEY_DIM], axis=-1)
    rep = A_V_HEADS // A_K_HEADS
    q = jnp.repeat(_l2norm(q.reshape(B, T, A_K_HEADS, A_HEAD_DIM)) * (A_HEAD_DIM ** -0.5), rep, axis=2)
    k = jnp.repeat(_l2norm(k.reshape(B, T, A_K_HEADS, A_HEAD_DIM)), rep, axis=2)
    v = v.reshape(B, T, A_V_HEADS, A_HEAD_DIM)
    beta = jax.nn.sigmoid(b)
    g = -jnp.exp(a_log) * jax.nn.softplus(a + dt_bias)
    o, S = _gated_delta(q, k, v, beta, g, S0)
    o = _rmsnorm(o, norm_g) * jax.nn.silu(z.reshape(B, T, A_V_HEADS, A_HEAD_DIM))
    return o.reshape(B * T, A_VAL_DIM), new_buf, S


def _complex_affine_combine(e1, e2):
    a1r, a1i, b1r, b1i = e1
    a2r, a2i, b2r, b2i = e2
    return (a2r * a1r - a2i * a1i, a2r * a1i + a2i * a1r,
            a2r * b1r - a2i * b1i + b2r, a2r * b1i + a2i * b1r + b2i)


def _mixer_b_core(u, s_re, s_im, lam_re, lam_im, log_dt, b_re, b_im, c_re, c_im, d_skip):
    B, T, _ = u.shape
    ug = u.reshape(B, T, B_GROUPS, B_GROUP)
    dt = jnp.exp(log_dt)[:, None]
    mag = jnp.exp(lam_re * dt)
    ab_re, ab_im = mag * jnp.cos(lam_im * dt), mag * jnp.sin(lam_im * dt)
    den = lam_re * lam_re + lam_im * lam_im
    f_re = ((ab_re - 1.0) * lam_re + ab_im * lam_im) / den
    f_im = (ab_im * lam_re - (ab_re - 1.0) * lam_im) / den
    bb_re = f_re[..., None] * b_re - f_im[..., None] * b_im
    bb_im = f_re[..., None] * b_im + f_im[..., None] * b_re
    bu_re = jnp.einsum('btgc,gpc->btgp', ug, bb_re)
    bu_im = jnp.einsum('btgc,gpc->btgp', ug, bb_im)
    bu_re = bu_re.at[:, 0].add(ab_re * s_re - ab_im * s_im)
    bu_im = bu_im.at[:, 0].add(ab_re * s_im + ab_im * s_re)
    a_re = jnp.broadcast_to(ab_re, bu_re.shape)
    a_im = jnp.broadcast_to(ab_im, bu_im.shape)
    _, _, x_re, x_im = lax.associative_scan(_complex_affine_combine, (a_re, a_im, bu_re, bu_im), axis=1)
    y = jnp.einsum('btgp,gcp->btgc', x_re, c_re) - jnp.einsum('btgp,gcp->btgc', x_im, c_im)
    y = y.reshape(B, T, D_MODEL) + d_skip * u
    y = jax.nn.gelu(y)
    return y.reshape(B * T, D_MODEL), x_re[:, -1], x_im[:, -1]


def _attend(s, v, spec):
    m = jnp.max(s, axis=-1, keepdims=True)
    p = jnp.exp(s - m)
    l = jnp.sum(p, axis=-1)
    o = jnp.einsum(spec, p, v) / jnp.swapaxes(l, -1, -2)[..., None]
    lse = jnp.swapaxes(m[..., 0] + jnp.log(l), -1, -2)
    return o, lse


def _dilated_prompt(q, k, v, dil, nw):
    B, T, H, hd = q.shape
    L = T // dil
    nb = -(-L // nw)
    lp = nb * nw
    z = B * dil

    def by_residue(a):
        return a.reshape(B, L, dil, H, hd).transpose(0, 2, 1, 3, 4).reshape(z, L, H, hd)

    qs = jnp.pad(by_residue(q), ((0, 0), (0, lp - L), (0, 0), (0, 0))).reshape(z, nb, nw, H, hd)

    def key_blocks(a):
        ap = jnp.pad(by_residue(a), ((0, 0), (nw, lp - L), (0, 0), (0, 0)))
        return jnp.concatenate([ap[:, :lp].reshape(z, nb, nw, H, hd),
                                ap[:, nw:].reshape(z, nb, nw, H, hd)], axis=2)

    kb, vb = key_blocks(k), key_blocks(v)
    qi = jnp.arange(nw)[:, None]
    ks = jnp.arange(2 * nw)[None, :]
    dist = qi + nw - ks
    first = (jnp.arange(nb) * nw - nw)[:, None, None] + ks[None]
    mask = (dist >= 0) & (dist <= nw) & (first >= 0)
    s = jnp.einsum('znqhd,znkhd->znhqk', qs, kb) * (hd ** -0.5)
    s = jnp.where(mask[None, :, None], s, -jnp.inf)
    o, lse = _attend(s, vb, 'znhqk,znkhd->znqhd')
    o = o.reshape(z, lp, H, hd)[:, :L].reshape(B, dil, L, H, hd).transpose(0, 2, 1, 3, 4).reshape(B, T, H, hd)
    lse = lse.reshape(z, lp, H)[:, :L].reshape(B, dil, L, H).transpose(0, 2, 1, 3).reshape(B, T, H)
    return o, lse


def _dilated_sample(q, k, v, k_buf, v_buf, dil, nw):
    Tn = q.shape[1]
    lb = k_buf.shape[1]
    kc = jnp.concatenate([k_buf.astype(k.dtype), k], axis=1)
    vc = jnp.concatenate([v_buf.astype(v.dtype), v], axis=1)
    idx = lb + jnp.arange(Tn)[:, None] - dil * jnp.arange(nw + 1)[None, :]
    valid = idx >= 0
    idx = jnp.maximum(idx, 0)
    kg = jnp.take(kc, idx, axis=1)
    vg = jnp.take(vc, idx, axis=1)
    s = jnp.einsum('bqhd,bqkhd->bhqk', q, kg) * (q.shape[-1] ** -0.5)
    s = jnp.where(valid[None, None], s, -jnp.inf)
    o, lse = _attend(s, vg, 'bhqk,bqkhd->bqhd')
    return o, lse, kc[:, -lb:], vc[:, -lb:]


def _mixer_c_core(qkv, bufs, q_norm_g, k_norm_g):
    B, T, _ = qkv.shape
    qkv = qkv.reshape(B, T, C_GROUPS, 3, C_HEADS, C_HEAD_DIM)
    q = _rmsnorm(qkv[:, :, :, 0], q_norm_g[:, None, :])
    k = _rmsnorm(qkv[:, :, :, 1], k_norm_g[:, None, :])
    v = qkv[:, :, :, 2]
    outs, lses, new_bufs = [], [], []
    for gi in range(C_GROUPS):
        dil = C_DILATIONS[gi]
        nw = C_WINDOWS[gi] // dil
        qg, kg, vg = q[:, :, gi], k[:, :, gi], v[:, :, gi]
        if bufs is None:
            o, lse = _dilated_prompt(qg, kg, vg, dil, nw)
            lb = min(C_WINDOWS[gi], T)
            nk, nv = kg[:, T - lb:], vg[:, T - lb:]
        else:
            o, lse, nk, nv = _dilated_sample(qg, kg, vg, bufs[gi][0], bufs[gi][1], dil, nw)
        outs.append(o)
        lses.append(lse)
        new_bufs.append((nk, nv))
    wts = jax.nn.softmax(jnp.stack(lses), axis=0)
    o = jnp.sum(wts[..., None] * jnp.stack(outs), axis=0)
    return o.reshape(B * T, D_MODEL), new_bufs


def _mixer_c_prompt(h, B, T, w_qkv, slot, q_norm_g, k_norm_g, tm):
    W = D_MODEL
    outs, lses, new_bufs = [], [], []
    for gi in range(C_GROUPS):
        dil = C_DILATIONS[gi]
        nw = C_WINDOWS[gi] // dil
        L = T // dil
        assert L % nw == 0

        def to_res(a, w, dil=dil, L=L):
            return a.reshape(B, L, dil, w).transpose(0, 2, 1, 3).reshape(B * T, w)

        def from_res(a, w, dil=dil, L=L):
            return a.reshape(B, dil, L, w).transpose(0, 2, 1, 3).reshape(B, T, w)

        hp = h if dil == 1 else to_res(h, W)
        qkv = _linear(hp, w_qkv, (slot,), n_out=3 * W, tm=tm, tn=512, col_off=gi * 3 * W // 512)
        o, kn, lse = _band_attn(qkv, q_norm_g[gi], k_norm_g[gi], z=B * dil, L=L,
                                heads=C_HEADS, hd=C_HEAD_DIM, nw=nw)
        lb = min(C_WINDOWS[gi], T)
        outs.append(from_res(o, W).reshape(B, T, C_HEADS, C_HEAD_DIM))
        lses.append(from_res(lse, LANE)[..., :C_HEADS])
        nk = from_res(kn, W)[:, T - lb:].reshape(B, lb, C_HEADS, C_HEAD_DIM)
        nv = from_res(qkv[:, 2 * W:], W)[:, T - lb:].reshape(B, lb, C_HEADS, C_HEAD_DIM)
        new_bufs.append((nk, nv))
    wts = jax.nn.softmax(jnp.stack(lses), axis=0)
    o = jnp.sum(wts[..., None] * jnp.stack(outs), axis=0)
    return o.reshape(B * T, D_MODEL), new_bufs


def _route(logits, tm):
    M = logits.shape[0]
    top_v, top_i = lax.top_k(logits, TOP_K)
    gates = jax.nn.softmax(top_v, axis=-1)
    e_flat = top_i.reshape(-1)
    onehot = (e_flat[:, None] == jnp.arange(N_EXPERTS)[None, :]).astype(jnp.int32)
    counts = jnp.sum(onehot, axis=0)
    rank = jnp.sum((jnp.cumsum(onehot, axis=0) - onehot) * onehot, axis=1)
    padded = ((counts + tm - 1) // tm) * tm
    ends = jnp.cumsum(padded)
    pos = (ends - padded)[e_flat] + rank
    nt = (TOP_K * M + N_EXPERTS * (tm - 1)) // tm
    src = jnp.zeros((nt * tm,), jnp.int32).at[pos].set(jnp.arange(TOP_K * M, dtype=jnp.int32) // TOP_K)
    starts = jnp.arange(nt, dtype=jnp.int32) * tm
    tile_expert = jnp.sum((starts[:, None] >= ends[None, :]).astype(jnp.int32), axis=1)
    tile_valid = (starts < ends[-1]).astype(jnp.int32)
    last_expert = tile_expert[ends[-1] // tm - 1]
    tile_expert = jnp.where(tile_valid > 0, tile_expert, last_expert).astype(jnp.int32)
    return gates, pos.reshape(M, TOP_K), src, tile_expert, tile_valid


def _trunk(x, mod, st, p, *, tm, tm_moe):
    B, T, _ = x.shape
    M = B * T
    per_row = tm > T

    def vec(v):
        if per_row:
            return jnp.repeat(v, T, axis=0).reshape(1, M, D_MODEL)
        return v.reshape(B, 1, D_MODEL)

    tm_big_k = min(tm, 512)
    x = x.reshape(M, D_MODEL)
    a_conv, a_rec, b_re, b_im = [], [], [], []
    c_k = [[] for _ in range(C_GROUPS)]
    c_v = [[] for _ in range(C_GROUPS)]
    for i in range(DEPTH):
        sh1, sc1, gt1, sh2, sc2, gt2 = (vec(m) for m in jnp.split(mod[i], N_MOD, axis=-1))
        kind, slot = i % N_MIXERS, i // N_MIXERS
        if kind == 0:
            h = _modulate(x, p['g_mix'][i], sh1, sc1, tm=tm)[0]
            main = _linear(h, p['a_w_in'], (slot,), n_out=A_CONV_DIM + A_VAL_DIM, tm=tm, tn=512)
            small = _linear(h, p['a_w_in'], (slot,), n_out=LANE, tm=tm, tn=LANE,
                            col_off=(A_CONV_DIM + A_VAL_DIM) // LANE)
            qkv = main[:, :A_CONV_DIM].reshape(B, T, A_CONV_DIM)
            z = main[:, A_CONV_DIM:].reshape(B, T, A_VAL_DIM)
            bb = small[:, :A_V_HEADS].reshape(B, T, A_V_HEADS)
            aa = small[:, A_V_HEADS:2 * A_V_HEADS].reshape(B, T, A_V_HEADS)
            if st is None:
                buf0 = jnp.zeros((B, A_CONV - 1, A_CONV_DIM), F32)
                s0 = jnp.zeros((B, A_V_HEADS, A_HEAD_DIM, A_HEAD_DIM), F32)
            else:
                buf0, s0 = st['a_conv'][slot], st['a_rec'][slot]
            o, nbuf, ns = _mixer_a_core(qkv, z, bb, aa, buf0, s0, p['a_conv_w'][slot], p['a_log'][slot],
                                        p['a_dt_bias'][slot], p['a_norm_g'][slot])
            a_conv.append(nbuf)
            a_rec.append(ns)
            x = _linear(o.astype(BF16), p['a_w_out'], (slot,), n_out=D_MODEL, tm=tm_big_k, tn=512,
                        mode="resid", res=x, gate=gt1)
        elif kind == 1:
            h, hf = _modulate(x, p['g_mix'][i], sh1, sc1, tm=tm, extra="f32")
            if st is None:
                r0 = jnp.zeros((B, B_GROUPS, B_STATE), F32)
                m0 = jnp.zeros((B, B_GROUPS, B_STATE), F32)
            else:
                r0, m0 = st['b_re'][slot], st['b_im'][slot]
            y, nr, ni = _s5_core(hf.reshape(B, T, D_MODEL), r0, m0, p['b_lambda_re'][slot],
                                 p['b_lambda_im'][slot], p['b_log_dt'][slot], p['b_B_re'][slot],
                                 p['b_B_im'][slot], p['b_C_re'][slot], p['b_C_im'][slot], p['b_D'][slot],
                                 tc=min(T, 32))
            b_re.append(nr)
            b_im.append(ni)
            x = _linear(y, p['b_w_glu'], (slot,), n_out=D_MODEL, tm=tm, tn=512,
                        mode="glu_resid", col_off2=D_MODEL // 512, res=x, gate=gt1)
        else:
            h = _modulate(x, p['g_mix'][i], sh1, sc1, tm=tm)[0]
            if st is None:
                o, nbs = _mixer_c_prompt(h, B, T, p['c_w_qkv'], slot, p['c_q_norm_g'][slot],
                                         p['c_k_norm_g'][slot], tm)
            else:
                qkv = _linear(h, p['c_w_qkv'], (slot,), n_out=C_GROUPS * 3 * D_MODEL, tm=tm, tn=512)
                bufs = [(st['c_k'][g][slot], st['c_v'][g][slot]) for g in range(C_GROUPS)]
                o, nbs = _mixer_c_core(qkv.reshape(B, T, -1), bufs, p['c_q_norm_g'][slot],
                                       p['c_k_norm_g'][slot])
            for g in range(C_GROUPS):
                c_k[g].append(nbs[g][0])
                c_v[g].append(nbs[g][1])
            x = _linear(o.astype(BF16), p['c_w_out'], (slot,), n_out=D_MODEL, tm=tm, tn=512,
                        mode="resid", res=x, gate=gt1)
        if i % 2 == 0:
            h = _modulate(x, p['g_ffn'][i], sh2, sc2, tm=tm)[0]
            act = _linear(h, p['f_w_gate_up'], (i // 2,), n_out=F_DENSE, tm=tm, tn=512, mode="swiglu",
                          col_off2=F_DENSE // 512, out_dtype=BF16)
            x = _linear(act, p['f_w_down'], (i // 2,), n_out=D_MODEL, tm=tm_big_k, tn=512,
                        mode="resid", res=x, gate=gt2)
        else:
            router = jnp.pad(p['m_router'][i // 2], ((0, 0), (0, LANE - N_EXPERTS)))
            h, logits = _modulate(x, p['g_ffn'][i], sh2, sc2, tm=tm, extra="router", router=router)
            gates, pos, src, tile_expert, tile_valid = _route(logits[:, :N_EXPERTS], tm_moe)
            xs = jnp.take(h, src, axis=0)
            act = _moe_linear(xs, p['m_w_gate_up'], i // 2, tile_expert, tile_valid,
                              n_out=F_EXPERT, tm=tm_moe, tn=512, up=True)
            ys = _moe_linear(act, p['m_w_down'], i // 2, tile_expert, tile_valid,
                             n_out=D_MODEL, tm=tm_moe, tn=512, up=False)
            y = (gates[:, 0:1] * jnp.take(ys, pos[:, 0], axis=0)
                 + gates[:, 1:2] * jnp.take(ys, pos[:, 1], axis=0))
            gt2_rows = gt2.reshape(M, D_MODEL) if per_row else jnp.repeat(gt2.reshape(B, D_MODEL), T, axis=0)
            x = x + gt2_rows * y
    stk = lambda lst: jnp.stack(lst, axis=0)
    states = (stk(a_conv), stk(a_rec), stk(b_re), stk(b_im), stk(c_k[0]), stk(c_v[0]),
              stk(c_k[1]), stk(c_v[1]), stk(c_k[2]), stk(c_v[2]))
    return x.reshape(B, T, D_MODEL), states


def kernel(x_prompt, x_sample, c_prompt, c_sample, state_a_conv, state_a_rec, state_b_re, state_b_im, cache_c_k0, cache_c_v0, cache_c_k1, cache_c_v1, cache_c_k2, cache_c_v2, g_mix, g_ffn, w_mod, b_mod, a_w_in, a_conv_w, a_log, a_dt_bias, a_norm_g, a_w_out, b_lambda_re, b_lambda_im, b_log_dt, b_B_re, b_B_im, b_C_re, b_C_im, b_D, b_w_glu, c_w_qkv, c_q_norm_g, c_k_norm_g, c_w_out, f_w_gate_up, f_w_down, m_router, m_w_gate_up, m_w_down):
    p = dict(g_mix=g_mix, g_ffn=g_ffn,
             a_w_in=a_w_in, a_conv_w=a_conv_w, a_log=a_log, a_dt_bias=a_dt_bias, a_norm_g=a_norm_g, a_w_out=a_w_out,
             b_lambda_re=b_lambda_re, b_lambda_im=b_lambda_im, b_log_dt=b_log_dt, b_B_re=b_B_re, b_B_im=b_B_im,
             b_C_re=b_C_re, b_C_im=b_C_im, b_D=b_D, b_w_glu=b_w_glu,
             c_w_qkv=c_w_qkv, c_q_norm_g=c_q_norm_g, c_k_norm_g=c_k_norm_g, c_w_out=c_w_out,
             f_w_gate_up=f_w_gate_up, f_w_down=f_w_down,
             m_router=m_router, m_w_gate_up=m_w_gate_up, m_w_down=m_w_down)
    st = dict(a_conv=state_a_conv, a_rec=state_a_rec, b_re=state_b_re, b_im=state_b_im,
              c_k=(cache_c_k0, cache_c_k1, cache_c_k2), c_v=(cache_c_v0, cache_c_v1, cache_c_v2))
    nb_p, nb_s = c_prompt.shape[0], c_sample.shape[0]
    c_all = jnp.concatenate([c_prompt, c_sample, jnp.zeros((16 - nb_p - nb_s, D_MODEL), F32)], axis=0)
    mod = _adaln(c_all, w_mod, b_mod)
    y_prompt, ps = _trunk(x_prompt, mod[:, :nb_p], None, p, tm=1024, tm_moe=256)
    y_sample, ss = _trunk(x_sample, mod[:, nb_p:nb_p + nb_s], st, p, tm=64, tm_moe=16)
    return (y_prompt, y_sample) + ps + ss
```

```python
import functools

import jax
import jax.numpy as jnp
from jax import lax
from jax.experimental import pallas as pl
from jax.experimental.pallas import tpu as pltpu

D_MODEL = 2048
DEPTH = 4
N_MIXERS = 3
N_MOD = 6
EPS = 1e-6

A_K_HEADS = 16
A_V_HEADS = 32
A_HEAD_DIM = 128
A_KEY_DIM = A_K_HEADS * A_HEAD_DIM
A_VAL_DIM = A_V_HEADS * A_HEAD_DIM
A_CONV_DIM = 2 * A_KEY_DIM + A_VAL_DIM
A_CONV = 4
A_CHUNK = 64

B_GROUP = 16
B_GROUPS = D_MODEL // B_GROUP
B_STATE = 64

C_HEADS = 16
C_HEAD_DIM = D_MODEL // C_HEADS
C_WINDOWS = (128, 512, 2048)
C_DILATIONS = (1, 4, 16)
C_GROUPS = 3

F_DENSE = 5632
N_EXPERTS = 8
TOP_K = 2
F_EXPERT = 7168

LANE = 128
VMEM_LIMIT = 56 * 1024 * 1024
BF16 = jnp.bfloat16
F32 = jnp.float32


def _params(n_grid):
    return pltpu.CompilerParams(dimension_semantics=("arbitrary",) * n_grid,
                                vmem_limit_bytes=VMEM_LIMIT)


def _cast_weights_on_first_row_tile(w_refs, wb_refs):
    @pl.when(pl.program_id(1) == 0)
    def _():
        for w_ref, wb_ref in zip(w_refs, wb_refs):
            wb_ref[...] = w_ref[...].astype(BF16)


def _dot(x, wb_ref):
    return jnp.dot(x, wb_ref[...], preferred_element_type=F32)


def _k_plain(x_ref, w_ref, o_ref, wb_ref):
    _cast_weights_on_first_row_tile((w_ref,), (wb_ref,))
    o_ref[...] = _dot(x_ref[...], wb_ref).astype(o_ref.dtype)


def _k_resid(x_ref, w_ref, r_ref, g_ref, o_ref, wb_ref):
    _cast_weights_on_first_row_tile((w_ref,), (wb_ref,))
    o_ref[...] = r_ref[...] + g_ref[...] * _dot(x_ref[...], wb_ref)


def _k_swiglu(x_ref, wg_ref, wu_ref, o_ref, wgb_ref, wub_ref):
    _cast_weights_on_first_row_tile((wg_ref, wu_ref), (wgb_ref, wub_ref))
    x = x_ref[...]
    g = _dot(x, wgb_ref)
    u = _dot(x, wub_ref)
    o_ref[...] = (g * jax.nn.sigmoid(g) * u).astype(o_ref.dtype)


def _k_glu_resid(x_ref, wa_ref, wb_ref, r_ref, g_ref, o_ref, wab_ref, wbb_ref):
    _cast_weights_on_first_row_tile((wa_ref, wb_ref), (wab_ref, wbb_ref))
    x = x_ref[...]
    ya = _dot(x, wab_ref)
    yb = _dot(x, wbb_ref)
    o_ref[...] = r_ref[...] + g_ref[...] * (ya * jax.nn.sigmoid(yb))


def _linear(x, w, widx, *, n_out, tm, tn, mode="plain", col_off=0, col_off2=0,
            res=None, gate=None, out_dtype=F32):
    M, K = x.shape
    assert M % tm == 0 and n_out % tn == 0
    ni, nj = M // tm, n_out // tn
    lead = (None,) * len(widx)

    def wspec(off):
        return pl.BlockSpec(lead + (K, tn), lambda j, i: tuple(widx) + (0, j + off))

    xspec = pl.BlockSpec((tm, K), lambda j, i: (i, 0))
    ospec = pl.BlockSpec((tm, tn), lambda j, i: (i, j))
    two = mode in ("swiglu", "glu_resid")
    in_specs = [xspec, wspec(col_off)] + ([wspec(col_off2)] if two else [])
    args = [x, w] + ([w] if two else [])
    if mode in ("resid", "glu_resid"):
        tiles_per_gate = ni // gate.shape[0]
        in_specs += [ospec,
                     pl.BlockSpec((None, gate.shape[1], tn),
                                  lambda j, i: (i // tiles_per_gate, 0, j))]
        args += [res, gate]
    body = {"plain": _k_plain, "resid": _k_resid, "swiglu": _k_swiglu,
            "glu_resid": _k_glu_resid}[mode]
    return pl.pallas_call(
        body,
        grid=(nj, ni),
        in_specs=in_specs,
        out_specs=ospec,
        out_shape=jax.ShapeDtypeStruct((M, n_out), out_dtype),
        scratch_shapes=[pltpu.VMEM((K, tn), BF16)] * (2 if two else 1),
        compiler_params=_params(2),
        name="linear_" + mode,
    )(*args)


def _cast_weights_on_expert_change(te_ref, w_refs, wb_refs):
    t = pl.program_id(1)
    prev = te_ref[jnp.maximum(t - 1, 0)]

    @pl.when((t == 0) | (te_ref[t] != prev))
    def _():
        for w_ref, wb_ref in zip(w_refs, wb_refs):
            wb_ref[...] = w_ref[...].astype(BF16)


def _k_moe_up(te_ref, tv_ref, x_ref, wg_ref, wu_ref, o_ref, wgb_ref, wub_ref):
    _cast_weights_on_expert_change(te_ref, (wg_ref, wu_ref), (wgb_ref, wub_ref))
    valid = tv_ref[pl.program_id(1)] > 0

    @pl.when(valid)
    def _():
        x = x_ref[...]
        g = _dot(x, wgb_ref)
        u = _dot(x, wub_ref)
        o_ref[...] = (g * jax.nn.sigmoid(g) * u).astype(o_ref.dtype)

    @pl.when(jnp.logical_not(valid))
    def _():
        o_ref[...] = jnp.zeros_like(o_ref)


def _k_moe_down(te_ref, tv_ref, x_ref, w_ref, o_ref, wb_ref):
    _cast_weights_on_expert_change(te_ref, (w_ref,), (wb_ref,))
    valid = tv_ref[pl.program_id(1)] > 0

    @pl.when(valid)
    def _():
        o_ref[...] = _dot(x_ref[...], wb_ref)

    @pl.when(jnp.logical_not(valid))
    def _():
        o_ref[...] = jnp.zeros_like(o_ref)


def _moe_linear(x, w, layer, tile_expert, tile_valid, *, n_out, tm, tn, up):
    Mp, K = x.shape
    nt, nj = Mp // tm, n_out // tn

    def wspec(off):
        return pl.BlockSpec((None, None, K, tn),
                            lambda j, t, te, tv: (layer, te[t], 0, j + off))

    xspec = pl.BlockSpec((tm, K), lambda j, t, te, tv: (t, 0))
    ospec = pl.BlockSpec((tm, tn), lambda j, t, te, tv: (t, j))
    in_specs = [xspec, wspec(0)] + ([wspec(nj)] if up else [])
    args = [x, w] + ([w] if up else [])
    return pl.pallas_call(
        _k_moe_up if up else _k_moe_down,
        grid_spec=pltpu.PrefetchScalarGridSpec(
            num_scalar_prefetch=2,
            grid=(nj, nt),
            in_specs=in_specs,
            out_specs=ospec,
            scratch_shapes=[pltpu.VMEM((K, tn), BF16)] * (2 if up else 1)),
        out_shape=jax.ShapeDtypeStruct((Mp, n_out), BF16 if up else F32),
        compiler_params=_params(2),
        name="moe_up" if up else "moe_down",
    )(tile_expert, tile_valid, *args)


def _k_adaln(c_ref, w_ref, b_ref, o_ref):
    c = c_ref[...]
    a = (c * jax.nn.sigmoid(c)).astype(BF16)
    o_ref[...] = jnp.dot(a, w_ref[...].astype(BF16), preferred_element_type=F32) + b_ref[...]


def _adaln(c_all, w_mod, b_mod):
    R = c_all.shape[0]
    tn = 1024
    n = N_MOD * D_MODEL
    return pl.pallas_call(
        _k_adaln,
        grid=(DEPTH, n // tn),
        in_specs=[pl.BlockSpec((R, D_MODEL), lambda l, j: (0, 0)),
                  pl.BlockSpec((None, D_MODEL, tn), lambda l, j: (l, 0, j)),
                  pl.BlockSpec((None, 1, tn), lambda l, j: (l, 0, j))],
        out_specs=pl.BlockSpec((None, R, tn), lambda l, j: (l, 0, j)),
        out_shape=jax.ShapeDtypeStruct((DEPTH, R, n), F32),
        compiler_params=_params(2),
        name="adaln",
    )(c_all, w_mod, b_mod.reshape(DEPTH, 1, n))


def _modulated(x_ref, g_ref, sh_ref, sc_ref):
    x = x_ref[...]
    y = x * lax.rsqrt(jnp.mean(x * x, axis=-1, keepdims=True) + EPS)
    return (y * g_ref[...]) * (1.0 + sc_ref[...]) + sh_ref[...]


def _k_modulate(x_ref, g_ref, sh_ref, sc_ref, hb_ref):
    hb_ref[...] = _modulated(x_ref, g_ref, sh_ref, sc_ref).astype(BF16)


def _k_modulate_f32(x_ref, g_ref, sh_ref, sc_ref, hb_ref, hf_ref):
    h = _modulated(x_ref, g_ref, sh_ref, sc_ref)
    hb_ref[...] = h.astype(BF16)
    hf_ref[...] = h


def _k_modulate_router(x_ref, g_ref, sh_ref, sc_ref, r_ref, hb_ref, lg_ref):
    h = _modulated(x_ref, g_ref, sh_ref, sc_ref)
    hb = h.astype(BF16)
    hb_ref[...] = hb
    lg_ref[...] = jnp.dot(hb, r_ref[...].astype(BF16), preferred_element_type=F32)


def _modulate(x, g, shift, scale, *, tm, extra=None, router=None):
    M = x.shape[0]
    ni = M // tm
    tiles_per_vec = ni // shift.shape[0]
    xspec = pl.BlockSpec((tm, D_MODEL), lambda i: (i, 0))
    vspec = pl.BlockSpec((None, shift.shape[1], D_MODEL), lambda i: (i // tiles_per_vec, 0, 0))
    in_specs = [xspec, pl.BlockSpec((1, D_MODEL), lambda i: (0, 0)), vspec, vspec]
    args = [x, g.reshape(1, D_MODEL), shift, scale]
    out_specs = [xspec]
    out_shape = [jax.ShapeDtypeStruct((M, D_MODEL), BF16)]
    body = _k_modulate
    if extra == "f32":
        body = _k_modulate_f32
        out_specs.append(xspec)
        out_shape.append(jax.ShapeDtypeStruct((M, D_MODEL), F32))
    elif extra == "router":
        body = _k_modulate_router
        in_specs.append(pl.BlockSpec((D_MODEL, LANE), lambda i: (0, 0)))
        args.append(router)
        out_specs.append(pl.BlockSpec((tm, LANE), lambda i: (i, 0)))
        out_shape.append(jax.ShapeDtypeStruct((M, LANE), F32))
    return pl.pallas_call(
        body, grid=(ni,), in_specs=in_specs, out_specs=out_specs, out_shape=out_shape,
        compiler_params=_params(1), name="modulate",
    )(*args)


S5_SUPER = 16
S5_ROWS = 8


def _k_s5(u_ref, bb_ref, cre_ref, cim_ref, are_ref, aim_ref, d_ref, s0re_ref, s0im_ref,
          y_ref, fre_ref, fim_ref, xre_ref, xim_ref, car_re, car_im, *, tc):
    @pl.when(pl.program_id(0) == 0)
    def _():
        car_re[...] = s0re_ref[...]
        car_im[...] = s0im_ref[...]

    ub = u_ref[...].astype(BF16)
    for s in range(S5_SUPER):
        bu = jnp.dot(ub[:, s * 128:(s + 1) * 128], bb_ref[s], preferred_element_type=F32)
        xre_ref[:, s * 512:(s + 1) * 512] = bu[:, :512]
        xim_ref[:, s * 512:(s + 1) * 512] = bu[:, 512:]

    for s in range(S5_SUPER):
        cols = slice(s * 512, (s + 1) * 512)
        a_re = jnp.broadcast_to(are_ref[:, cols], (S5_ROWS, 512))
        a_im = jnp.broadcast_to(aim_ref[:, cols], (S5_ROWS, 512))

        def body(t, carry, cols=cols, a_re=a_re, a_im=a_im):
            xr, xi = carry
            r0 = pl.multiple_of(t * S5_ROWS, S5_ROWS)
            nr = (a_re * xr - a_im * xi) + xre_ref[pl.ds(r0, S5_ROWS), cols]
            ni = (a_re * xi + a_im * xr) + xim_ref[pl.ds(r0, S5_ROWS), cols]
            xre_ref[pl.ds(r0, S5_ROWS), cols] = nr
            xim_ref[pl.ds(r0, S5_ROWS), cols] = ni
            return nr, ni

        xr, xi = lax.fori_loop(0, tc, body, (car_re[:, cols], car_im[:, cols]))
        car_re[:, cols] = xr
        car_im[:, cols] = xi

    for s in range(S5_SUPER):
        cols = slice(s * 512, (s + 1) * 512)
        lanes = slice(s * 128, (s + 1) * 128)
        y = (jnp.dot(xre_ref[:, cols].astype(BF16), cre_ref[s], preferred_element_type=F32)
             - jnp.dot(xim_ref[:, cols].astype(BF16), cim_ref[s], preferred_element_type=F32))
        y = y + d_ref[:, lanes] * u_ref[:, lanes]
        y_ref[:, lanes] = jax.nn.gelu(y).astype(BF16)
    fre_ref[...] = car_re[...]
    fim_ref[...] = car_im[...]


def _s5_core(u, s_re, s_im, lam_re, lam_im, log_dt, b_re, b_im, c_re, c_im, d_skip, *, tc):
    B, T, D = u.shape
    G, P = lam_re.shape
    Cg = D // G
    assert G == S5_SUPER * 8 and 8 * Cg == LANE and B <= S5_ROWS and T % tc == 0
    dt = jnp.exp(log_dt)[:, None]
    mag = jnp.exp(lam_re * dt)
    ab_re, ab_im = mag * jnp.cos(lam_im * dt), mag * jnp.sin(lam_im * dt)
    den = lam_re * lam_re + lam_im * lam_im
    f_re = ((ab_re - 1.0) * lam_re + ab_im * lam_im) / den
    f_im = (ab_im * lam_re - (ab_re - 1.0) * lam_im) / den
    bb_re = f_re[..., None] * b_re - f_im[..., None] * b_im
    bb_im = f_re[..., None] * b_im + f_im[..., None] * b_re
    eye = jnp.eye(8, dtype=F32)

    def blockdiag_in(bb):
        t = bb.reshape(S5_SUPER, 8, P, Cg)
        return jnp.einsum('sgpc,gh->sgchp', t, eye).reshape(S5_SUPER, 8 * Cg, 8 * P)

    def blockdiag_out(c):
        t = c.reshape(S5_SUPER, 8, Cg, P)
        return jnp.einsum('sgcp,gh->sgphc', t, eye).reshape(S5_SUPER, 8 * P, 8 * Cg)

    bb = jnp.concatenate([blockdiag_in(bb_re), blockdiag_in(bb_im)], axis=-1).astype(BF16)
    cre = blockdiag_out(c_re).astype(BF16)
    cim = blockdiag_out(c_im).astype(BF16)
    pad = S5_ROWS - B
    u_tb = jnp.pad(jnp.transpose(u, (1, 0, 2)), ((0, 0), (0, pad), (0, 0))).reshape(T * S5_ROWS, D)
    s0re = jnp.pad(s_re.reshape(B, G * P), ((0, pad), (0, 0)))
    s0im = jnp.pad(s_im.reshape(B, G * P), ((0, pad), (0, 0)))
    R = tc * S5_ROWS
    NS = G * P
    full = lambda shp: pl.BlockSpec(shp, lambda i: (0,) * len(shp))
    y, fre, fim = pl.pallas_call(
        functools.partial(_k_s5, tc=tc),
        grid=(T // tc,),
        in_specs=[pl.BlockSpec((R, D), lambda i: (i, 0)),
                  full((S5_SUPER, 8 * Cg, 16 * P)), full((S5_SUPER, 8 * P, 8 * Cg)),
                  full((S5_SUPER, 8 * P, 8 * Cg)),
                  full((1, NS)), full((1, NS)), full((1, D)), full((S5_ROWS, NS)), full((S5_ROWS, NS))],
        out_specs=[pl.BlockSpec((R, D), lambda i: (i, 0)), full((S5_ROWS, NS)), full((S5_ROWS, NS))],
        out_shape=[jax.ShapeDtypeStruct((T * S5_ROWS, D), BF16),
                   jax.ShapeDtypeStruct((S5_ROWS, NS), F32), jax.ShapeDtypeStruct((S5_ROWS, NS), F32)],
        scratch_shapes=[pltpu.VMEM((R, NS), F32), pltpu.VMEM((R, NS), F32),
                        pltpu.VMEM((S5_ROWS, NS), F32), pltpu.VMEM((S5_ROWS, NS), F32)],
        compiler_params=_params(1),
        name="s5_core",
    )(u_tb, bb, cre, cim, ab_re.reshape(1, NS), ab_im.reshape(1, NS), d_skip.reshape(1, D), s0re, s0im)
    y = jnp.transpose(y.reshape(T, S5_ROWS, D)[:, :B], (1, 0, 2)).reshape(B * T, D)
    return y, fre[:B].reshape(B, G, P), fim[:B].reshape(B, G, P)


def _k_band_attn(q_ref, kc_ref, kp_ref, vc_ref, vp_ref, qg_ref, kg_ref, o_ref, kn_ref, lse_ref,
                 *, heads, hd, nw):
    bi = pl.program_id(1)
    qi = lax.broadcasted_iota(jnp.int32, (nw, nw), 0)
    kj = lax.broadcasted_iota(jnp.int32, (nw, nw), 1)
    cur_ok = kj <= qi
    prev_ok = jnp.logical_and(kj >= qi, bi > 0)
    lane = lax.broadcasted_iota(jnp.int32, (nw, LANE), 1)
    scale = hd ** -0.5
    qg = qg_ref[...]
    kg = kg_ref[...]
    lse_tile = jnp.zeros((nw, LANE), F32)

    def rms(x, g):
        return (x * lax.rsqrt(jnp.mean(x * x, axis=-1, keepdims=True) + EPS)) * g

    nt = (((1,), (1,)), ((), ()))
    for h in range(heads):
        sl = slice(h * hd, (h + 1) * hd)
        q = rms(q_ref[:, sl], qg).astype(BF16)
        kc = rms(kc_ref[:, sl], kg)
        kn_ref[:, sl] = kc
        kp = rms(kp_ref[:, sl], kg).astype(BF16)
        s_c = lax.dot_general(q, kc.astype(BF16), nt, preferred_element_type=F32) * scale
        s_p = lax.dot_general(q, kp, nt, preferred_element_type=F32) * scale
        s_c = jnp.where(cur_ok, s_c, -jnp.inf)
        s_p = jnp.where(prev_ok, s_p, -jnp.inf)
        m = jnp.maximum(jnp.max(s_c, axis=-1, keepdims=True), jnp.max(s_p, axis=-1, keepdims=True))
        p_c = jnp.exp(s_c - m)
        p_p = jnp.exp(s_p - m)
        l = jnp.sum(p_c, axis=-1, keepdims=True) + jnp.sum(p_p, axis=-1, keepdims=True)
        o = (jnp.dot(p_p.astype(BF16), vp_ref[:, sl].astype(BF16), preferred_element_type=F32)
             + jnp.dot(p_c.astype(BF16), vc_ref[:, sl].astype(BF16), preferred_element_type=F32))
        o_ref[:, sl] = o / l
        lse_tile = jnp.where(lane == h, m + jnp.log(l), lse_tile)
    lse_ref[...] = lse_tile


def _band_attn(qkv, q_g, k_g, *, z, L, heads, hd, nw):
    W = heads * hd
    nb = L // nw
    x = qkv.reshape(z, L, 3 * W)

    def blk(col, prev):
        if prev:
            return pl.BlockSpec((None, nw, W), lambda zi, bi: (zi, jnp.maximum(bi - 1, 0), col))
        return pl.BlockSpec((None, nw, W), lambda zi, bi: (zi, bi, col))

    gspec = pl.BlockSpec((1, hd), lambda zi, bi: (0, 0))
    ospec = pl.BlockSpec((None, nw, W), lambda zi, bi: (zi, bi, 0))
    lspec = pl.BlockSpec((None, nw, LANE), lambda zi, bi: (zi, bi, 0))
    o, kn, lse = pl.pallas_call(
        functools.partial(_k_band_attn, heads=heads, hd=hd, nw=nw),
        grid=(z, nb),
        in_specs=[blk(0, False), blk(1, False), blk(1, True), blk(2, False), blk(2, True), gspec, gspec],
        out_specs=[ospec, ospec, lspec],
        out_shape=[jax.ShapeDtypeStruct((z, L, W), F32), jax.ShapeDtypeStruct((z, L, W), F32),
                   jax.ShapeDtypeStruct((z, L, LANE), F32)],
        compiler_params=_params(2),
        name="band_attn",
    )(x, x, x, x, x, q_g.reshape(1, hd), k_g.reshape(1, hd))
    return o.reshape(z * L, W), kn.reshape(z * L, W), lse.reshape(z * L, LANE)


DELTA_PACK = 4
DELTA_GROUPS = 4


def _split_bf16(x):
    hi = x.astype(BF16)
    return hi, (x - hi.astype(F32)).astype(BF16)


def _dot3(a, b):
    d = lambda x, y: jnp.dot(x, y, preferred_element_type=F32)
    return d(a[0], b[0]) + (d(a[0], b[1]) + d(a[1], b[0]))


def _k_delta(q_ref, k_ref, v_ref, z_ref, gcol_ref, grow_ref, s0_ref, ng_ref, o_ref, s_ref, ablk_ref,
             *, c, hd):
    @pl.when(pl.program_id(2) == 0)
    def _():
        s_ref[...] = s0_ref[...]
        ablk_ref[...] = jnp.zeros_like(ablk_ref)

    ri = lax.broadcasted_iota(jnp.int32, (c, c), 0)
    ci = lax.broadcasted_iota(jnp.int32, (c, c), 1)
    causal = ci <= ri
    strict = ci < ri
    nt = (((1,), (1,)), ((), ()))
    tn = (((0,), (0,)), ((), ()))
    ng = ng_ref[...]
    heads, a_all, x_all = [], [], []
    for g in range(DELTA_GROUPS):
        rhs_rows = []
        for i in range(DELTA_PACK):
            kh = (g * DELTA_PACK + i) // 2
            col = g * LANE + i
            beta = gcol_ref[:, col:col + 1]
            gc = gcol_ref[:, col + DELTA_PACK:col + DELTA_PACK + 1]
            gr = grow_ref[g, i:i + 1, :]
            q = q_ref[:, kh * hd:(kh + 1) * hd]
            k = k_ref[:, kh * hd:(kh + 1) * hd]
            v = v_ref[:, (g * DELTA_PACK + i) * hd:(g * DELTA_PACK + i + 1) * hd]
            decay = jnp.where(causal, jnp.exp(jnp.where(causal, gc - gr, 0.0)), 0.0)
            kb = k * beta
            kbf = k.astype(BF16)
            kk = lax.dot_general(kb.astype(BF16), kbf, nt, preferred_element_type=F32)
            ablk_ref[g, i * c:(i + 1) * c, i * c:(i + 1) * c] = jnp.where(strict, kk * decay, 0.0)
            qk = lax.dot_general(q.astype(BF16), kbf, nt, preferred_element_type=F32) * decay
            egc = jnp.exp(gc)
            rhs_rows.append(jnp.concatenate([v * beta, kb * egc], axis=1))
            heads.append((q, k, gc, egc, qk))
        a = _split_bf16(ablk_ref[g])
        r = jnp.concatenate(rhs_rows, axis=0)
        a_all.append(a)
        x_all.append(r - _dot3(a, _split_bf16(r)))
    p_all = a_all
    m = 1
    while 2 * m < c:
        p_all = [_split_bf16(_dot3(p, p)) for p in p_all]
        x_all = [x + _dot3(p, _split_bf16(x)) for p, x in zip(p_all, x_all)]
        m *= 2
    for g in range(DELTA_GROUPS):
        x = x_all[g]
        for i in range(DELTA_PACK):
            hidx = g * DELTA_PACK + i
            q, k, gc, egc, qk = heads[hidx]
            u = x[i * c:(i + 1) * c, :hd]
            w = x[i * c:(i + 1) * c, hd:]
            s = s_ref[hidx]
            sb = s.astype(BF16)
            v_new = u - jnp.dot(w.astype(BF16), sb, preferred_element_type=F32)
            vnb = v_new.astype(BF16)
            o = (jnp.dot((q * egc).astype(BF16), sb, preferred_element_type=F32)
                 + jnp.dot(qk.astype(BF16), vnb, preferred_element_type=F32))
            g_last = gc[c - 1:c, :]
            kd = (k * jnp.exp(g_last - gc)).astype(BF16)
            s_ref[hidx] = s * jnp.exp(g_last) + lax.dot_general(kd, vnb, tn, preferred_element_type=F32)
            on = (o * lax.rsqrt(jnp.mean(o * o, axis=-1, keepdims=True) + EPS)) * ng
            zz = z_ref[:, hidx * hd:(hidx + 1) * hd]
            o_ref[:, hidx * hd:(hidx + 1) * hd] = (on * (zz * jax.nn.sigmoid(zz))).astype(BF16)


def _delta_core(q, k, v, z, z_col_off, beta, g, S0, norm_g, *, c=A_CHUNK):
    B, Tp, VH = beta.shape
    hd = S0.shape[-1]
    KH = q.shape[-1] // hd
    assert Tp % c == 0 and VH == 2 * KH and VH % (DELTA_PACK * DELTA_GROUPS) == 0
    nchunk = Tp // c
    npair = VH // (DELTA_PACK * DELTA_GROUPS)
    ngrp = VH // DELTA_PACK
    gc = jnp.cumsum(g.reshape(B, nchunk, c, VH), axis=2)
    gcol = jnp.concatenate([beta.reshape(B, nchunk, c, ngrp, DELTA_PACK),
                            gc.reshape(B, nchunk, c, ngrp, DELTA_PACK),
                            jnp.zeros((B, nchunk, c, ngrp, LANE - 2 * DELTA_PACK), F32)], axis=-1)
    gcol = gcol.reshape(B * Tp, ngrp * LANE)
    grow = jnp.transpose(gc, (0, 1, 3, 2)).reshape(B, nchunk, ngrp, DELTA_PACK, c)
    grow = jnp.pad(grow, ((0, 0), (0, 0), (0, 0), (0, 8 - DELTA_PACK), (0, 0)))
    M = B * Tp
    kw = DELTA_GROUPS * DELTA_PACK // 2 * hd
    vw = DELTA_GROUPS * DELTA_PACK * hd
    row = lambda b, hp, n: b * nchunk + n
    return pl.pallas_call(
        functools.partial(_k_delta, c=c, hd=hd),
        grid=(B, npair, nchunk),
        in_specs=[pl.BlockSpec((c, kw), lambda b, hp, n: (row(b, hp, n), hp)),
                  pl.BlockSpec((c, kw), lambda b, hp, n: (row(b, hp, n), hp)),
                  pl.BlockSpec((c, vw), lambda b, hp, n: (row(b, hp, n), hp)),
                  pl.BlockSpec((c, vw), lambda b, hp, n: (row(b, hp, n), hp + z_col_off)),
                  pl.BlockSpec((c, DELTA_GROUPS * LANE), lambda b, hp, n: (row(b, hp, n), hp)),
                  pl.BlockSpec((None, None, DELTA_GROUPS, 8, c), lambda b, hp, n: (b, n, hp, 0, 0)),
                  pl.BlockSpec((None, DELTA_GROUPS * DELTA_PACK, hd, hd), lambda b, hp, n: (b, hp, 0, 0)),
                  pl.BlockSpec((1, hd), lambda b, hp, n: (0, 0))],
        out_specs=[pl.BlockSpec((c, vw), lambda b, hp, n: (row(b, hp, n), hp)),
                   pl.BlockSpec((None, DELTA_GROUPS * DELTA_PACK, hd, hd), lambda b, hp, n: (b, hp, 0, 0))],
        out_shape=[jax.ShapeDtypeStruct((M, VH * hd), BF16),
                   jax.ShapeDtypeStruct(S0.shape, F32)],
        scratch_shapes=[pltpu.VMEM((DELTA_GROUPS, DELTA_PACK * c, DELTA_PACK * c), F32)],
        compiler_params=_params(3),
        name="delta_core",
    )(q.reshape(M, KH * hd), k.reshape(M, KH * hd), v.reshape(M, VH * hd), z, gcol, grow, S0,
      norm_g.reshape(1, hd))


def _rmsnorm(x, g):
    xf = x.astype(F32)
    y = xf * lax.rsqrt(jnp.mean(xf * xf, axis=-1, keepdims=True) + EPS)
    return (y * g.astype(F32)).astype(x.dtype)


def _l2norm(x):
    return x * lax.rsqrt(jnp.sum(x * x, axis=-1, keepdims=True) + EPS)


def _short_conv(u, buf, w):
    T = u.shape[1]
    up = jnp.concatenate([buf.astype(u.dtype), u], axis=1)
    y = up[:, 0:T] * w[0]
    for j in range(1, A_CONV):
        y = y + up[:, j:j + T] * w[j]
    return jax.nn.silu(y), up[:, T:]


def _mixer_a_core(main, small, conv_buf, S0, conv_w, a_log, dt_bias, norm_g, B, T):
    qkv = main[:, :A_CONV_DIM].reshape(B, T, A_CONV_DIM)
    b = small[:, :A_V_HEADS].reshape(B, T, A_V_HEADS)
    a = small[:, A_V_HEADS:2 * A_V_HEADS].reshape(B, T, A_V_HEADS)
    qkv, new_buf = _short_conv(qkv, conv_buf, conv_w)
    q, k, v = jnp.split(qkv, [A_KEY_DIM, 2 * A_KEY_DIM], axis=-1)
    q = (_l2norm(q.reshape(B, T, A_K_HEADS, A_HEAD_DIM)) * (A_HEAD_DIM ** -0.5)).reshape(B, T, A_KEY_DIM)
    k = _l2norm(k.reshape(B, T, A_K_HEADS, A_HEAD_DIM)).reshape(B, T, A_KEY_DIM)
    beta = jax.nn.sigmoid(b)
    g = -jnp.exp(a_log) * jax.nn.softplus(a + dt_bias)
    Tp = -(-T // A_CHUNK) * A_CHUNK
    if Tp == T:
        z, z_col_off = main, A_CONV_DIM // (DELTA_GROUPS * DELTA_PACK * A_HEAD_DIM)
    else:
        padt = lambda t: jnp.pad(t, ((0, 0), (0, Tp - T), (0, 0)))
        q, k, v, beta, g = (padt(t) for t in (q, k, v, beta, g))
        z = padt(main[:, A_CONV_DIM:].reshape(B, T, A_VAL_DIM)).reshape(B * Tp, A_VAL_DIM)
        z_col_off = 0
    o, S = _delta_core(q, k, v, z, z_col_off, beta, g, S0, norm_g)
    if Tp != T:
        o = o.reshape(B, Tp, A_VAL_DIM)[:, :T].reshape(B * T, A_VAL_DIM)
    return o, new_buf, S


def _attend(s, v, spec):
    m = jnp.max(s, axis=-1, keepdims=True)
    p = jnp.exp(s - m)
    l = jnp.sum(p, axis=-1)
    o = jnp.einsum(spec, p, v) / jnp.swapaxes(l, -1, -2)[..., None]
    lse = jnp.swapaxes(m[..., 0] + jnp.log(l), -1, -2)
    return o, lse


def _dilated_prompt(q, k, v, dil, nw):
    B, T, H, hd = q.shape
    L = T // dil
    nb = -(-L // nw)
    lp = nb * nw
    z = B * dil

    def by_residue(a):
        return a.reshape(B, L, dil, H, hd).transpose(0, 2, 1, 3, 4).reshape(z, L, H, hd)

    qs = jnp.pad(by_residue(q), ((0, 0), (0, lp - L), (0, 0), (0, 0))).reshape(z, nb, nw, H, hd)

    def key_blocks(a):
        ap = jnp.pad(by_residue(a), ((0, 0), (nw, lp - L), (0, 0), (0, 0)))
        return jnp.concatenate([ap[:, :lp].reshape(z, nb, nw, H, hd),
                                ap[:, nw:].reshape(z, nb, nw, H, hd)], axis=2)

    kb, vb = key_blocks(k), key_blocks(v)
    qi = jnp.arange(nw)[:, None]
    ks = jnp.arange(2 * nw)[None, :]
    dist = qi + nw - ks
    first = (jnp.arange(nb) * nw - nw)[:, None, None] + ks[None]
    mask = (dist >= 0) & (dist <= nw) & (first >= 0)
    s = jnp.einsum('znqhd,znkhd->znhqk', qs, kb) * (hd ** -0.5)
    s = jnp.where(mask[None, :, None], s, -jnp.inf)
    o, lse = _attend(s, vb, 'znhqk,znkhd->znqhd')
    o = o.reshape(z, lp, H, hd)[:, :L].reshape(B, dil, L, H, hd).transpose(0, 2, 1, 3, 4).reshape(B, T, H, hd)
    lse = lse.reshape(z, lp, H)[:, :L].reshape(B, dil, L, H).transpose(0, 2, 1, 3).reshape(B, T, H)
    return o, lse


def _dilated_sample(q, k, v, k_buf, v_buf, dil, nw):
    Tn = q.shape[1]
    lb = k_buf.shape[1]
    kc = jnp.concatenate([k_buf.astype(k.dtype), k], axis=1)
    vc = jnp.concatenate([v_buf.astype(v.dtype), v], axis=1)
    idx = lb + jnp.arange(Tn)[:, None] - dil * jnp.arange(nw + 1)[None, :]
    valid = idx >= 0
    idx = jnp.maximum(idx, 0)
    kg = jnp.take(kc, idx, axis=1)
    vg = jnp.take(vc, idx, axis=1)
    s = jnp.einsum('bqhd,bqkhd->bhqk', q, kg) * (q.shape[-1] ** -0.5)
    s = jnp.where(valid[None, None], s, -jnp.inf)
    o, lse = _attend(s, vg, 'bhqk,bqkhd->bqhd')
    return o, lse, kc[:, -lb:], vc[:, -lb:]


def _mixer_c_core(qkv, bufs, q_norm_g, k_norm_g):
    B, T, _ = qkv.shape
    qkv = qkv.reshape(B, T, C_GROUPS, 3, C_HEADS, C_HEAD_DIM)
    q = _rmsnorm(qkv[:, :, :, 0], q_norm_g[:, None, :])
    k = _rmsnorm(qkv[:, :, :, 1], k_norm_g[:, None, :])
    v = qkv[:, :, :, 2]
    outs, lses, new_bufs = [], [], []
    for gi in range(C_GROUPS):
        dil = C_DILATIONS[gi]
        nw = C_WINDOWS[gi] // dil
        qg, kg, vg = q[:, :, gi], k[:, :, gi], v[:, :, gi]
        if bufs is None:
            o, lse = _dilated_prompt(qg, kg, vg, dil, nw)
            lb = min(C_WINDOWS[gi], T)
            nk, nv = kg[:, T - lb:], vg[:, T - lb:]
        else:
            o, lse, nk, nv = _dilated_sample(qg, kg, vg, bufs[gi][0], bufs[gi][1], dil, nw)
        outs.append(o)
        lses.append(lse)
        new_bufs.append((nk, nv))
    wts = jax.nn.softmax(jnp.stack(lses), axis=0)
    o = jnp.sum(wts[..., None] * jnp.stack(outs), axis=0)
    return o.reshape(B * T, D_MODEL), new_bufs


def _mixer_c_prompt(h, B, T, w_qkv, slot, q_norm_g, k_norm_g, tm):
    W = D_MODEL
    outs, lses, new_bufs = [], [], []
    for gi in range(C_GROUPS):
        dil = C_DILATIONS[gi]
        nw = C_WINDOWS[gi] // dil
        L = T // dil
        assert L % nw == 0

        def to_res(a, w, dil=dil, L=L):
            return a.reshape(B, L, dil, w).transpose(0, 2, 1, 3).reshape(B * T, w)

        def from_res(a, w, dil=dil, L=L):
            return a.reshape(B, dil, L, w).transpose(0, 2, 1, 3).reshape(B, T, w)

        hp = h if dil == 1 else to_res(h, W)
        qkv = _linear(hp, w_qkv, (slot,), n_out=3 * W, tm=tm, tn=512, col_off=gi * 3 * W // 512)
        o, kn, lse = _band_attn(qkv, q_norm_g[gi], k_norm_g[gi], z=B * dil, L=L,
                                heads=C_HEADS, hd=C_HEAD_DIM, nw=nw)
        lb = min(C_WINDOWS[gi], T)
        outs.append(from_res(o, W).reshape(B, T, C_HEADS, C_HEAD_DIM))
        lses.append(from_res(lse, LANE)[..., :C_HEADS])
        nk = from_res(kn, W)[:, T - lb:].reshape(B, lb, C_HEADS, C_HEAD_DIM)
        nv = from_res(qkv[:, 2 * W:], W)[:, T - lb:].reshape(B, lb, C_HEADS, C_HEAD_DIM)
        new_bufs.append((nk, nv))
    wts = jax.nn.softmax(jnp.stack(lses), axis=0)
    o = jnp.sum(wts[..., None] * jnp.stack(outs), axis=0)
    return o.reshape(B * T, D_MODEL), new_bufs


def _route(logits, tm):
    M = logits.shape[0]
    top_v, top_i = lax.top_k(logits, TOP_K)
    gates = jax.nn.softmax(top_v, axis=-1)
    e_flat = top_i.reshape(-1)
    onehot = (e_flat[:, None] == jnp.arange(N_EXPERTS)[None, :]).astype(jnp.int32)
    counts = jnp.sum(onehot, axis=0)
    rank = jnp.sum((jnp.cumsum(onehot, axis=0) - onehot) * onehot, axis=1)
    padded = ((counts + tm - 1) // tm) * tm
    ends = jnp.cumsum(padded)
    pos = (ends - padded)[e_flat] + rank
    nt = (TOP_K * M + N_EXPERTS * (tm - 1)) // tm
    src = jnp.zeros((nt * tm,), jnp.int32).at[pos].set(jnp.arange(TOP_K * M, dtype=jnp.int32) // TOP_K)
    starts = jnp.arange(nt, dtype=jnp.int32) * tm
    tile_expert = jnp.sum((starts[:, None] >= ends[None, :]).astype(jnp.int32), axis=1)
    tile_valid = (starts < ends[-1]).astype(jnp.int32)
    last_expert = tile_expert[ends[-1] // tm - 1]
    tile_expert = jnp.where(tile_valid > 0, tile_expert, last_expert).astype(jnp.int32)
    return gates, pos.reshape(M, TOP_K), src, tile_expert, tile_valid


def _trunk(x, mod, st, p, *, tm, tm_moe, tm_moe_down):
    B, T, _ = x.shape
    M = B * T
    per_row = tm > T

    def vec(v):
        if per_row:
            return jnp.repeat(v, T, axis=0).reshape(1, M, D_MODEL)
        return v.reshape(B, 1, D_MODEL)

    tm_big_k = min(tm, 512)
    x = x.reshape(M, D_MODEL)
    a_conv, a_rec, b_re, b_im = [], [], [], []
    c_k = [[] for _ in range(C_GROUPS)]
    c_v = [[] for _ in range(C_GROUPS)]
    for i in range(DEPTH):
        sh1, sc1, gt1, sh2, sc2, gt2 = (vec(m) for m in jnp.split(mod[i], N_MOD, axis=-1))
        kind, slot = i % N_MIXERS, i // N_MIXERS
        if kind == 0:
            h = _modulate(x, p['g_mix'][i], sh1, sc1, tm=tm)[0]
            main = _linear(h, p['a_w_in'], (slot,), n_out=A_CONV_DIM + A_VAL_DIM, tm=tm, tn=512)
            w_small = jnp.pad(p['a_w_in'][slot, :, A_CONV_DIM + A_VAL_DIM:],
                              ((0, 0), (0, LANE - 2 * A_V_HEADS)))
            small = _linear(h, w_small[None], (0,), n_out=LANE, tm=tm, tn=LANE)
            if st is None:
                buf0 = jnp.zeros((B, A_CONV - 1, A_CONV_DIM), F32)
                s0 = jnp.zeros((B, A_V_HEADS, A_HEAD_DIM, A_HEAD_DIM), F32)
            else:
                buf0, s0 = st['a_conv'][slot], st['a_rec'][slot]
            o, nbuf, ns = _mixer_a_core(main, small, buf0, s0, p['a_conv_w'][slot], p['a_log'][slot],
                                        p['a_dt_bias'][slot], p['a_norm_g'][slot], B, T)
            a_conv.append(nbuf)
            a_rec.append(ns)
            x = _linear(o, p['a_w_out'], (slot,), n_out=D_MODEL, tm=tm_big_k, tn=512,
                        mode="resid", res=x, gate=gt1)
        elif kind == 1:
            h, hf = _modulate(x, p['g_mix'][i], sh1, sc1, tm=tm, extra="f32")
            if st is None:
                r0 = jnp.zeros((B, B_GROUPS, B_STATE), F32)
                m0 = jnp.zeros((B, B_GROUPS, B_STATE), F32)
            else:
                r0, m0 = st['b_re'][slot], st['b_im'][slot]
            y, nr, ni = _s5_core(hf.reshape(B, T, D_MODEL), r0, m0, p['b_lambda_re'][slot],
                                 p['b_lambda_im'][slot], p['b_log_dt'][slot], p['b_B_re'][slot],
                                 p['b_B_im'][slot], p['b_C_re'][slot], p['b_C_im'][slot], p['b_D'][slot],
                                 tc=min(T, 32))
            b_re.append(nr)
            b_im.append(ni)
            x = _linear(y, p['b_w_glu'], (slot,), n_out=D_MODEL, tm=tm, tn=512,
                        mode="glu_resid", col_off2=D_MODEL // 512, res=x, gate=gt1)
        else:
            h = _modulate(x, p['g_mix'][i], sh1, sc1, tm=tm)[0]
            if st is None:
                o, nbs = _mixer_c_prompt(h, B, T, p['c_w_qkv'], slot, p['c_q_norm_g'][slot],
                                         p['c_k_norm_g'][slot], tm)
            else:
                qkv = _linear(h, p['c_w_qkv'], (slot,), n_out=C_GROUPS * 3 * D_MODEL, tm=tm, tn=512)
                bufs = [(st['c_k'][g][slot], st['c_v'][g][slot]) for g in range(C_GROUPS)]
                o, nbs = _mixer_c_core(qkv.reshape(B, T, -1), bufs, p['c_q_norm_g'][slot],
                                       p['c_k_norm_g'][slot])
            for g in range(C_GROUPS):
                c_k[g].append(nbs[g][0])
                c_v[g].append(nbs[g][1])
            x = _linear(o.astype(BF16), p['c_w_out'], (slot,), n_out=D_MODEL, tm=tm, tn=512,
                        mode="resid", res=x, gate=gt1)
        if i % 2 == 0:
            h = _modulate(x, p['g_ffn'][i], sh2, sc2, tm=tm)[0]
            act = _linear(h, p['f_w_gate_up'], (i // 2,), n_out=F_DENSE, tm=tm, tn=512, mode="swiglu",
                          col_off2=F_DENSE // 512, out_dtype=BF16)
            x = _linear(act, p['f_w_down'], (i // 2,), n_out=D_MODEL, tm=tm_big_k, tn=512,
                        mode="resid", res=x, gate=gt2)
        else:
            router = jnp.pad(p['m_router'][i // 2], ((0, 0), (0, LANE - N_EXPERTS)))
            h, logits = _modulate(x, p['g_ffn'][i], sh2, sc2, tm=tm, extra="router", router=router)
            gates, pos, src, tile_expert, tile_valid = _route(logits[:, :N_EXPERTS], tm_moe)
            xs = jnp.take(h, src, axis=0)
            act = _moe_linear(xs, p['m_w_gate_up'], i // 2, tile_expert, tile_valid,
                              n_out=F_EXPERT, tm=tm_moe, tn=512, up=True)
            split = tm_moe // tm_moe_down
            ys = _moe_linear(act, p['m_w_down'], i // 2, jnp.repeat(tile_expert, split),
                             jnp.repeat(tile_valid, split), n_out=D_MODEL, tm=tm_moe_down, tn=512, up=False)
            y = (gates[:, 0:1] * jnp.take(ys, pos[:, 0], axis=0)
                 + gates[:, 1:2] * jnp.take(ys, pos[:, 1], axis=0))
            gt2_rows = gt2.reshape(M, D_MODEL) if per_row else jnp.repeat(gt2.reshape(B, D_MODEL), T, axis=0)
            x = x + gt2_rows * y
    stk = lambda lst: jnp.stack(lst, axis=0)
    states = (stk(a_conv), stk(a_rec), stk(b_re), stk(b_im), stk(c_k[0]), stk(c_v[0]),
              stk(c_k[1]), stk(c_v[1]), stk(c_k[2]), stk(c_v[2]))
    return x.reshape(B, T, D_MODEL), states


def kernel(x_prompt, x_sample, c_prompt, c_sample, state_a_conv, state_a_rec, state_b_re, state_b_im, cache_c_k0, cache_c_v0, cache_c_k1, cache_c_v1, cache_c_k2, cache_c_v2, g_mix, g_ffn, w_mod, b_mod, a_w_in, a_conv_w, a_log, a_dt_bias, a_norm_g, a_w_out, b_lambda_re, b_lambda_im, b_log_dt, b_B_re, b_B_im, b_C_re, b_C_im, b_D, b_w_glu, c_w_qkv, c_q_norm_g, c_k_norm_g, c_w_out, f_w_gate_up, f_w_down, m_router, m_w_gate_up, m_w_down):
    p = dict(g_mix=g_mix, g_ffn=g_ffn,
             a_w_in=a_w_in, a_conv_w=a_conv_w, a_log=a_log, a_dt_bias=a_dt_bias, a_norm_g=a_norm_g, a_w_out=a_w_out,
             b_lambda_re=b_lambda_re, b_lambda_im=b_lambda_im, b_log_dt=b_log_dt, b_B_re=b_B_re, b_B_im=b_B_im,
             b_C_re=b_C_re, b_C_im=b_C_im, b_D=b_D, b_w_glu=b_w_glu,
             c_w_qkv=c_w_qkv, c_q_norm_g=c_q_norm_g, c_k_norm_g=c_k_norm_g, c_w_out=c_w_out,
             f_w_gate_up=f_w_gate_up, f_w_down=f_w_down,
             m_router=m_router, m_w_gate_up=m_w_gate_up, m_w_down=m_w_down)
    st = dict(a_conv=state_a_conv, a_rec=state_a_rec, b_re=state_b_re, b_im=state_b_im,
              c_k=(cache_c_k0, cache_c_k1, cache_c_k2), c_v=(cache_c_v0, cache_c_v1, cache_c_v2))
    nb_p, nb_s = c_prompt.shape[0], c_sample.shape[0]
    c_all = jnp.concatenate([c_prompt, c_sample, jnp.zeros((16 - nb_p - nb_s, D_MODEL), F32)], axis=0)
    mod = _adaln(c_all, w_mod, b_mod)
    y_prompt, ps = _trunk(x_prompt, mod[:, :nb_p], None, p, tm=1024, tm_moe=512, tm_moe_down=256)
    y_sample, ss = _trunk(x_sample, mod[:, nb_p:nb_p + nb_s], st, p, tm=64, tm_moe=16, tm_moe_down=16)
    return (y_prompt, y_sample) + ps + ss
```

```python
import functools

import jax
import jax.numpy as jnp
from jax import lax
from jax.experimental import pallas as pl
from jax.experimental.pallas import tpu as pltpu

D_MODEL = 2048
DEPTH = 4
N_MIXERS = 3
N_MOD = 6
EPS = 1e-6

A_K_HEADS = 16
A_V_HEADS = 32
A_HEAD_DIM = 128
A_KEY_DIM = A_K_HEADS * A_HEAD_DIM
A_VAL_DIM = A_V_HEADS * A_HEAD_DIM
A_CONV_DIM = 2 * A_KEY_DIM + A_VAL_DIM
A_CONV = 4
A_CHUNK = 64

B_GROUP = 16
B_GROUPS = D_MODEL // B_GROUP
B_STATE = 64

C_HEADS = 16
C_HEAD_DIM = D_MODEL // C_HEADS
C_WINDOWS = (128, 512, 2048)
C_DILATIONS = (1, 4, 16)
C_GROUPS = 3

F_DENSE = 5632
N_EXPERTS = 8
TOP_K = 2
F_EXPERT = 7168

LANE = 128
VMEM_LIMIT = 56 * 1024 * 1024
BF16 = jnp.bfloat16
F32 = jnp.float32


def _params(n_grid):
    return pltpu.CompilerParams(dimension_semantics=("arbitrary",) * n_grid,
                                vmem_limit_bytes=VMEM_LIMIT)


def _cast_weights_on_first_row_tile(w_refs, wb_refs):
    @pl.when(pl.program_id(1) == 0)
    def _():
        for w_ref, wb_ref in zip(w_refs, wb_refs):
            wb_ref[...] = w_ref[...].astype(BF16)


def _dot(x, wb_ref):
    return jnp.dot(x, wb_ref[...], preferred_element_type=F32)


def _k_plain(x_ref, w_ref, o_ref, wb_ref):
    _cast_weights_on_first_row_tile((w_ref,), (wb_ref,))
    o_ref[...] = _dot(x_ref[...], wb_ref).astype(o_ref.dtype)


def _k_resid(x_ref, w_ref, r_ref, g_ref, o_ref, wb_ref):
    _cast_weights_on_first_row_tile((w_ref,), (wb_ref,))
    o_ref[...] = r_ref[...] + g_ref[...] * _dot(x_ref[...], wb_ref)


def _k_swiglu(x_ref, wg_ref, wu_ref, o_ref, wgb_ref, wub_ref):
    _cast_weights_on_first_row_tile((wg_ref, wu_ref), (wgb_ref, wub_ref))
    x = x_ref[...]
    g = _dot(x, wgb_ref)
    u = _dot(x, wub_ref)
    o_ref[...] = (g * jax.nn.sigmoid(g) * u).astype(o_ref.dtype)


def _k_glu_resid(x_ref, wa_ref, wb_ref, r_ref, g_ref, o_ref, wab_ref, wbb_ref):
    _cast_weights_on_first_row_tile((wa_ref, wb_ref), (wab_ref, wbb_ref))
    x = x_ref[...]
    ya = _dot(x, wab_ref)
    yb = _dot(x, wbb_ref)
    o_ref[...] = r_ref[...] + g_ref[...] * (ya * jax.nn.sigmoid(yb))


def _linear(x, w, widx, *, n_out, tm, tn, mode="plain", col_off=0, col_off2=0,
            res=None, gate=None, out_dtype=F32):
    M, K = x.shape
    assert M % tm == 0 and n_out % tn == 0
    ni, nj = M // tm, n_out // tn
    lead = (None,) * len(widx)

    def wspec(off):
        return pl.BlockSpec(lead + (K, tn), lambda j, i: tuple(widx) + (0, j + off))

    xspec = pl.BlockSpec((tm, K), lambda j, i: (i, 0))
    ospec = pl.BlockSpec((tm, tn), lambda j, i: (i, j))
    two = mode in ("swiglu", "glu_resid")
    in_specs = [xspec, wspec(col_off)] + ([wspec(col_off2)] if two else [])
    args = [x, w] + ([w] if two else [])
    if mode in ("resid", "glu_resid"):
        tiles_per_gate = ni // gate.shape[0]
        in_specs += [ospec,
                     pl.BlockSpec((None, gate.shape[1], tn),
                                  lambda j, i: (i // tiles_per_gate, 0, j))]
        args += [res, gate]
    body = {"plain": _k_plain, "resid": _k_resid, "swiglu": _k_swiglu,
            "glu_resid": _k_glu_resid}[mode]
    return pl.pallas_call(
        body,
        grid=(nj, ni),
        in_specs=in_specs,
        out_specs=ospec,
        out_shape=jax.ShapeDtypeStruct((M, n_out), out_dtype),
        scratch_shapes=[pltpu.VMEM((K, tn), BF16)] * (2 if two else 1),
        compiler_params=_params(2),
        name="linear_" + mode,
    )(*args)


def _cast_weights_on_expert_change(te_ref, w_refs, wb_refs):
    t = pl.program_id(1)
    prev = te_ref[jnp.maximum(t - 1, 0)]

    @pl.when((t == 0) | (te_ref[t] != prev))
    def _():
        for w_ref, wb_ref in zip(w_refs, wb_refs):
            wb_ref[...] = w_ref[...].astype(BF16)


def _k_moe_up(te_ref, tv_ref, x_ref, wg_ref, wu_ref, o_ref, wgb_ref, wub_ref):
    _cast_weights_on_expert_change(te_ref, (wg_ref, wu_ref), (wgb_ref, wub_ref))
    valid = tv_ref[pl.program_id(1)] > 0

    @pl.when(valid)
    def _():
        x = x_ref[...]
        g = _dot(x, wgb_ref)
        u = _dot(x, wub_ref)
        o_ref[...] = (g * jax.nn.sigmoid(g) * u).astype(o_ref.dtype)

    @pl.when(jnp.logical_not(valid))
    def _():
        o_ref[...] = jnp.zeros_like(o_ref)


def _k_moe_down(te_ref, tv_ref, x_ref, w_ref, o_ref, wb_ref):
    _cast_weights_on_expert_change(te_ref, (w_ref,), (wb_ref,))
    valid = tv_ref[pl.program_id(1)] > 0

    @pl.when(valid)
    def _():
        o_ref[...] = _dot(x_ref[...], wb_ref)

    @pl.when(jnp.logical_not(valid))
    def _():
        o_ref[...] = jnp.zeros_like(o_ref)


def _moe_linear(x, w, layer, tile_expert, tile_valid, *, n_out, tm, tn, up):
    Mp, K = x.shape
    nt, nj = Mp // tm, n_out // tn

    def wspec(off):
        return pl.BlockSpec((None, None, K, tn),
                            lambda j, t, te, tv: (layer, te[t], 0, j + off))

    xspec = pl.BlockSpec((tm, K), lambda j, t, te, tv: (t, 0))
    ospec = pl.BlockSpec((tm, tn), lambda j, t, te, tv: (t, j))
    in_specs = [xspec, wspec(0)] + ([wspec(nj)] if up else [])
    args = [x, w] + ([w] if up else [])
    return pl.pallas_call(
        _k_moe_up if up else _k_moe_down,
        grid_spec=pltpu.PrefetchScalarGridSpec(
            num_scalar_prefetch=2,
            grid=(nj, nt),
            in_specs=in_specs,
            out_specs=ospec,
            scratch_shapes=[pltpu.VMEM((K, tn), BF16)] * (2 if up else 1)),
        out_shape=jax.ShapeDtypeStruct((Mp, n_out), BF16 if up else F32),
        compiler_params=_params(2),
        name="moe_up" if up else "moe_down",
    )(tile_expert, tile_valid, *args)


def _k_adaln(c_ref, w_ref, b_ref, o_ref):
    c = c_ref[...]
    a = (c * jax.nn.sigmoid(c)).astype(BF16)
    o_ref[...] = jnp.dot(a, w_ref[...].astype(BF16), preferred_element_type=F32) + b_ref[...]


def _adaln(c_all, w_mod, b_mod):
    R = c_all.shape[0]
    tn = 1024
    n = N_MOD * D_MODEL
    return pl.pallas_call(
        _k_adaln,
        grid=(DEPTH, n // tn),
        in_specs=[pl.BlockSpec((R, D_MODEL), lambda l, j: (0, 0)),
                  pl.BlockSpec((None, D_MODEL, tn), lambda l, j: (l, 0, j)),
                  pl.BlockSpec((None, 1, tn), lambda l, j: (l, 0, j))],
        out_specs=pl.BlockSpec((None, R, tn), lambda l, j: (l, 0, j)),
        out_shape=jax.ShapeDtypeStruct((DEPTH, R, n), F32),
        compiler_params=_params(2),
        name="adaln",
    )(c_all, w_mod, b_mod.reshape(DEPTH, 1, n))


def _modulated(x_ref, g_ref, sh_ref, sc_ref):
    x = x_ref[...]
    y = x * lax.rsqrt(jnp.mean(x * x, axis=-1, keepdims=True) + EPS)
    return (y * g_ref[...]) * (1.0 + sc_ref[...]) + sh_ref[...]


def _k_modulate(x_ref, g_ref, sh_ref, sc_ref, hb_ref):
    hb_ref[...] = _modulated(x_ref, g_ref, sh_ref, sc_ref).astype(BF16)


def _k_modulate_f32(x_ref, g_ref, sh_ref, sc_ref, hb_ref, hf_ref):
    h = _modulated(x_ref, g_ref, sh_ref, sc_ref)
    hb_ref[...] = h.astype(BF16)
    hf_ref[...] = h


def _k_modulate_router(x_ref, g_ref, sh_ref, sc_ref, r_ref, hb_ref, lg_ref):
    h = _modulated(x_ref, g_ref, sh_ref, sc_ref)
    hb = h.astype(BF16)
    hb_ref[...] = hb
    lg_ref[...] = jnp.dot(hb, r_ref[...].astype(BF16), preferred_element_type=F32)


def _modulate(x, g, shift, scale, *, tm, extra=None, router=None):
    M = x.shape[0]
    ni = M // tm
    tiles_per_vec = ni // shift.shape[0]
    xspec = pl.BlockSpec((tm, D_MODEL), lambda i: (i, 0))
    vspec = pl.BlockSpec((None, shift.shape[1], D_MODEL), lambda i: (i // tiles_per_vec, 0, 0))
    in_specs = [xspec, pl.BlockSpec((1, D_MODEL), lambda i: (0, 0)), vspec, vspec]
    args = [x, g.reshape(1, D_MODEL), shift, scale]
    out_specs = [xspec]
    out_shape = [jax.ShapeDtypeStruct((M, D_MODEL), BF16)]
    body = _k_modulate
    if extra == "f32":
        body = _k_modulate_f32
        out_specs.append(xspec)
        out_shape.append(jax.ShapeDtypeStruct((M, D_MODEL), F32))
    elif extra == "router":
        body = _k_modulate_router
        in_specs.append(pl.BlockSpec((D_MODEL, LANE), lambda i: (0, 0)))
        args.append(router)
        out_specs.append(pl.BlockSpec((tm, LANE), lambda i: (i, 0)))
        out_shape.append(jax.ShapeDtypeStruct((M, LANE), F32))
    return pl.pallas_call(
        body, grid=(ni,), in_specs=in_specs, out_specs=out_specs, out_shape=out_shape,
        compiler_params=_params(1), name="modulate",
    )(*args)


S5_SUPER = 16
S5_ROWS = 8


def _k_s5(u_ref, bb_ref, cre_ref, cim_ref, are_ref, aim_ref, d_ref, s0re_ref, s0im_ref,
          y_ref, fre_ref, fim_ref, xre_ref, xim_ref, car_re, car_im, *, tc):
    @pl.when(pl.program_id(0) == 0)
    def _():
        car_re[...] = s0re_ref[...]
        car_im[...] = s0im_ref[...]

    ub = u_ref[...].astype(BF16)
    for s in range(S5_SUPER):
        bu = jnp.dot(ub[:, s * 128:(s + 1) * 128], bb_ref[s], preferred_element_type=F32)
        xre_ref[:, s * 512:(s + 1) * 512] = bu[:, :512]
        xim_ref[:, s * 512:(s + 1) * 512] = bu[:, 512:]

    for s in range(S5_SUPER):
        cols = slice(s * 512, (s + 1) * 512)
        a_re = jnp.broadcast_to(are_ref[:, cols], (S5_ROWS, 512))
        a_im = jnp.broadcast_to(aim_ref[:, cols], (S5_ROWS, 512))

        def body(t, carry, cols=cols, a_re=a_re, a_im=a_im):
            xr, xi = carry
            r0 = pl.multiple_of(t * S5_ROWS, S5_ROWS)
            nr = (a_re * xr - a_im * xi) + xre_ref[pl.ds(r0, S5_ROWS), cols]
            ni = (a_re * xi + a_im * xr) + xim_ref[pl.ds(r0, S5_ROWS), cols]
            xre_ref[pl.ds(r0, S5_ROWS), cols] = nr
            xim_ref[pl.ds(r0, S5_ROWS), cols] = ni
            return nr, ni

        xr, xi = lax.fori_loop(0, tc, body, (car_re[:, cols], car_im[:, cols]))
        car_re[:, cols] = xr
        car_im[:, cols] = xi

    for s in range(S5_SUPER):
        cols = slice(s * 512, (s + 1) * 512)
        lanes = slice(s * 128, (s + 1) * 128)
        y = (jnp.dot(xre_ref[:, cols].astype(BF16), cre_ref[s], preferred_element_type=F32)
             - jnp.dot(xim_ref[:, cols].astype(BF16), cim_ref[s], preferred_element_type=F32))
        y = y + d_ref[:, lanes] * u_ref[:, lanes]
        y_ref[:, lanes] = jax.nn.gelu(y).astype(BF16)
    fre_ref[...] = car_re[...]
    fim_ref[...] = car_im[...]


def _s5_core(u, s_re, s_im, lam_re, lam_im, log_dt, b_re, b_im, c_re, c_im, d_skip, *, tc):
    B, T, D = u.shape
    G, P = lam_re.shape
    Cg = D // G
    assert G == S5_SUPER * 8 and 8 * Cg == LANE and B <= S5_ROWS and T % tc == 0
    dt = jnp.exp(log_dt)[:, None]
    mag = jnp.exp(lam_re * dt)
    ab_re, ab_im = mag * jnp.cos(lam_im * dt), mag * jnp.sin(lam_im * dt)
    den = lam_re * lam_re + lam_im * lam_im
    f_re = ((ab_re - 1.0) * lam_re + ab_im * lam_im) / den
    f_im = (ab_im * lam_re - (ab_re - 1.0) * lam_im) / den
    bb_re = f_re[..., None] * b_re - f_im[..., None] * b_im
    bb_im = f_re[..., None] * b_im + f_im[..., None] * b_re
    eye = jnp.eye(8, dtype=F32)

    def blockdiag_in(bb):
        t = bb.reshape(S5_SUPER, 8, P, Cg)
        return jnp.einsum('sgpc,gh->sgchp', t, eye).reshape(S5_SUPER, 8 * Cg, 8 * P)

    def blockdiag_out(c):
        t = c.reshape(S5_SUPER, 8, Cg, P)
        return jnp.einsum('sgcp,gh->sgphc', t, eye).reshape(S5_SUPER, 8 * P, 8 * Cg)

    bb = jnp.concatenate([blockdiag_in(bb_re), blockdiag_in(bb_im)], axis=-1).astype(BF16)
    cre = blockdiag_out(c_re).astype(BF16)
    cim = blockdiag_out(c_im).astype(BF16)
    pad = S5_ROWS - B
    u_tb = jnp.pad(jnp.transpose(u, (1, 0, 2)), ((0, 0), (0, pad), (0, 0))).reshape(T * S5_ROWS, D)
    s0re = jnp.pad(s_re.reshape(B, G * P), ((0, pad), (0, 0)))
    s0im = jnp.pad(s_im.reshape(B, G * P), ((0, pad), (0, 0)))
    R = tc * S5_ROWS
    NS = G * P
    full = lambda shp: pl.BlockSpec(shp, lambda i: (0,) * len(shp))
    y, fre, fim = pl.pallas_call(
        functools.partial(_k_s5, tc=tc),
        grid=(T // tc,),
        in_specs=[pl.BlockSpec((R, D), lambda i: (i, 0)),
                  full((S5_SUPER, 8 * Cg, 16 * P)), full((S5_SUPER, 8 * P, 8 * Cg)),
                  full((S5_SUPER, 8 * P, 8 * Cg)),
                  full((1, NS)), full((1, NS)), full((1, D)), full((S5_ROWS, NS)), full((S5_ROWS, NS))],
        out_specs=[pl.BlockSpec((R, D), lambda i: (i, 0)), full((S5_ROWS, NS)), full((S5_ROWS, NS))],
        out_shape=[jax.ShapeDtypeStruct((T * S5_ROWS, D), BF16),
                   jax.ShapeDtypeStruct((S5_ROWS, NS), F32), jax.ShapeDtypeStruct((S5_ROWS, NS), F32)],
        scratch_shapes=[pltpu.VMEM((R, NS), F32), pltpu.VMEM((R, NS), F32),
                        pltpu.VMEM((S5_ROWS, NS), F32), pltpu.VMEM((S5_ROWS, NS), F32)],
        compiler_params=_params(1),
        name="s5_core",
    )(u_tb, bb, cre, cim, ab_re.reshape(1, NS), ab_im.reshape(1, NS), d_skip.reshape(1, D), s0re, s0im)
    y = jnp.transpose(y.reshape(T, S5_ROWS, D)[:, :B], (1, 0, 2)).reshape(B * T, D)
    return y, fre[:B].reshape(B, G, P), fim[:B].reshape(B, G, P)


def _k_band_attn(q_ref, kc_ref, kp_ref, vc_ref, vp_ref, qg_ref, kg_ref, o_ref, kn_ref, lse_ref,
                 *, heads, hd, nw):
    bi = pl.program_id(1)
    qi = lax.broadcasted_iota(jnp.int32, (nw, nw), 0)
    kj = lax.broadcasted_iota(jnp.int32, (nw, nw), 1)
    cur_ok = kj <= qi
    prev_ok = jnp.logical_and(kj >= qi, bi > 0)
    lane = lax.broadcasted_iota(jnp.int32, (nw, LANE), 1)
    scale = hd ** -0.5
    qg = qg_ref[...]
    kg = kg_ref[...]
    lse_tile = jnp.zeros((nw, LANE), F32)

    def rms(x, g):
        return (x * lax.rsqrt(jnp.mean(x * x, axis=-1, keepdims=True) + EPS)) * g

    nt = (((1,), (1,)), ((), ()))
    for h in range(heads):
        sl = slice(h * hd, (h + 1) * hd)
        q = rms(q_ref[:, sl], qg).astype(BF16)
        kc = rms(kc_ref[:, sl], kg)
        kn_ref[:, sl] = kc
        kp = rms(kp_ref[:, sl], kg).astype(BF16)
        s_c = lax.dot_general(q, kc.astype(BF16), nt, preferred_element_type=F32) * scale
        s_p = lax.dot_general(q, kp, nt, preferred_element_type=F32) * scale
        s_c = jnp.where(cur_ok, s_c, -jnp.inf)
        s_p = jnp.where(prev_ok, s_p, -jnp.inf)
        m = jnp.maximum(jnp.max(s_c, axis=-1, keepdims=True), jnp.max(s_p, axis=-1, keepdims=True))
        p_c = jnp.exp(s_c - m)
        p_p = jnp.exp(s_p - m)
        l = jnp.sum(p_c, axis=-1, keepdims=True) + jnp.sum(p_p, axis=-1, keepdims=True)
        o = (jnp.dot(p_p.astype(BF16), vp_ref[:, sl].astype(BF16), preferred_element_type=F32)
             + jnp.dot(p_c.astype(BF16), vc_ref[:, sl].astype(BF16), preferred_element_type=F32))
        o_ref[:, sl] = o / l
        lse_tile = jnp.where(lane == h, m + jnp.log(l), lse_tile)
    lse_ref[...] = lse_tile


def _band_attn(qkv, q_g, k_g, *, z, L, heads, hd, nw):
    W = heads * hd
    nb = L // nw
    x = qkv.reshape(z, L, 3 * W)

    def blk(col, prev):
        if prev:
            return pl.BlockSpec((None, nw, W), lambda zi, bi: (zi, jnp.maximum(bi - 1, 0), col))
        return pl.BlockSpec((None, nw, W), lambda zi, bi: (zi, bi, col))

    gspec = pl.BlockSpec((1, hd), lambda zi, bi: (0, 0))
    ospec = pl.BlockSpec((None, nw, W), lambda zi, bi: (zi, bi, 0))
    lspec = pl.BlockSpec((None, nw, LANE), lambda zi, bi: (zi, bi, 0))
    o, kn, lse = pl.pallas_call(
        functools.partial(_k_band_attn, heads=heads, hd=hd, nw=nw),
        grid=(z, nb),
        in_specs=[blk(0, False), blk(1, False), blk(1, True), blk(2, False), blk(2, True), gspec, gspec],
        out_specs=[ospec, ospec, lspec],
        out_shape=[jax.ShapeDtypeStruct((z, L, W), F32), jax.ShapeDtypeStruct((z, L, W), F32),
                   jax.ShapeDtypeStruct((z, L, LANE), F32)],
        compiler_params=_params(2),
        name="band_attn",
    )(x, x, x, x, x, q_g.reshape(1, hd), k_g.reshape(1, hd))
    return o.reshape(z * L, W), kn.reshape(z * L, W), lse.reshape(z * L, LANE)


DELTA_PACK = 4
DELTA_GROUPS = 4


def _split_bf16(x):
    hi = x.astype(BF16)
    return hi, (x - hi.astype(F32)).astype(BF16)


def _dot3(a, b):
    d = lambda x, y: jnp.dot(x, y, preferred_element_type=F32)
    return d(a[0], b[0]) + (d(a[0], b[1]) + d(a[1], b[0]))


def _k_delta(q_ref, k_ref, v_ref, z_ref, gcol_ref, grow_ref, s0_ref, ng_ref, o_ref, s_ref, ablk_ref,
             *, c, hd):
    @pl.when(pl.program_id(2) == 0)
    def _():
        s_ref[...] = s0_ref[...]
        ablk_ref[...] = jnp.zeros_like(ablk_ref)

    ri = lax.broadcasted_iota(jnp.int32, (c, c), 0)
    ci = lax.broadcasted_iota(jnp.int32, (c, c), 1)
    causal = ci <= ri
    strict = ci < ri
    nt = (((1,), (1,)), ((), ()))
    tn = (((0,), (0,)), ((), ()))
    ng = ng_ref[...]
    heads, a_all, x_all = [], [], []
    for g in range(DELTA_GROUPS):
        rhs_rows = []
        for i in range(DELTA_PACK):
            kh = (g * DELTA_PACK + i) // 2
            col = g * LANE + i
            beta = gcol_ref[:, col:col + 1]
            gc = gcol_ref[:, col + DELTA_PACK:col + DELTA_PACK + 1]
            gr = grow_ref[g, i:i + 1, :]
            q = q_ref[:, kh * hd:(kh + 1) * hd]
            k = k_ref[:, kh * hd:(kh + 1) * hd]
            v = v_ref[:, (g * DELTA_PACK + i) * hd:(g * DELTA_PACK + i + 1) * hd]
            decay = jnp.where(causal, jnp.exp(jnp.where(causal, gc - gr, 0.0)), 0.0)
            kb = k * beta
            kbf = k.astype(BF16)
            kk = lax.dot_general(kb.astype(BF16), kbf, nt, preferred_element_type=F32)
            ablk_ref[g, i * c:(i + 1) * c, i * c:(i + 1) * c] = jnp.where(strict, kk * decay, 0.0)
            qk = lax.dot_general(q.astype(BF16), kbf, nt, preferred_element_type=F32) * decay
            egc = jnp.exp(gc)
            rhs_rows.append(jnp.concatenate([v * beta, kb * egc], axis=1))
            heads.append((q, k, gc, egc, qk))
        a = _split_bf16(ablk_ref[g])
        r = jnp.concatenate(rhs_rows, axis=0)
        a_all.append(a)
        x_all.append(r - _dot3(a, _split_bf16(r)))
    p_all = a_all
    m = 1
    while 2 * m < c:
        p_all = [_split_bf16(_dot3(p, p)) for p in p_all]
        x_all = [x + _dot3(p, _split_bf16(x)) for p, x in zip(p_all, x_all)]
        m *= 2
    for g in range(DELTA_GROUPS):
        x = x_all[g]
        for i in range(DELTA_PACK):
            hidx = g * DELTA_PACK + i
            q, k, gc, egc, qk = heads[hidx]
            u = x[i * c:(i + 1) * c, :hd]
            w = x[i * c:(i + 1) * c, hd:]
            s = s_ref[hidx]
            sb = s.astype(BF16)
            v_new = u - jnp.dot(w.astype(BF16), sb, preferred_element_type=F32)
            vnb = v_new.astype(BF16)
            o = (jnp.dot((q * egc).astype(BF16), sb, preferred_element_type=F32)
                 + jnp.dot(qk.astype(BF16), vnb, preferred_element_type=F32))
            g_last = gc[c - 1:c, :]
            kd = (k * jnp.exp(g_last - gc)).astype(BF16)
            s_ref[hidx] = s * jnp.exp(g_last) + lax.dot_general(kd, vnb, tn, preferred_element_type=F32)
            on = (o * lax.rsqrt(jnp.mean(o * o, axis=-1, keepdims=True) + EPS)) * ng
            zz = z_ref[:, hidx * hd:(hidx + 1) * hd]
            o_ref[:, hidx * hd:(hidx + 1) * hd] = (on * (zz * jax.nn.sigmoid(zz))).astype(BF16)


def _k_delta_prep(cur_ref, prev_ref, buf_ref, w_ref, o_ref, up_ref, *, tm, hd, n_q_tiles):
    halo = up_ref.shape[0] - tm
    first = pl.program_id(1) == 0
    up_ref[0:halo, :] = jnp.where(first, buf_ref[...], prev_ref[...])
    up_ref[halo:halo + tm, :] = cur_ref[...]
    start = halo - (A_CONV - 1)
    y = up_ref[start:start + tm, :] * w_ref[0:1, :]
    for j in range(1, A_CONV):
        y = y + up_ref[start + j:start + j + tm, :] * w_ref[j:j + 1, :]
    y = y * jax.nn.sigmoid(y)
    col = pl.program_id(2)
    scale = jnp.where(col < n_q_tiles, hd ** -0.5, 1.0)
    is_qk = col < 2 * n_q_tiles
    for h in range(cur_ref.shape[1] // hd):
        yh = y[:, h * hd:(h + 1) * hd]
        nh = (yh * lax.rsqrt(jnp.sum(yh * yh, axis=-1, keepdims=True) + EPS)) * scale
        o_ref[:, h * hd:(h + 1) * hd] = jnp.where(is_qk, nh, yh)


PREP_HALO = 8


def _delta_prep(main, conv_buf, conv_w, B, T, *, tm, cw=512):
    assert T % tm == 0 and tm % PREP_HALO == 0 and A_KEY_DIM % cw == 0
    buf8 = jnp.pad(conv_buf, ((0, 0), (PREP_HALO - (A_CONV - 1), 0), (0, 0)))
    nrow = T // tm
    per_halo = tm // PREP_HALO
    return pl.pallas_call(
        functools.partial(_k_delta_prep, tm=tm, hd=A_HEAD_DIM, n_q_tiles=A_KEY_DIM // cw),
        grid=(B, nrow, A_CONV_DIM // cw),
        in_specs=[pl.BlockSpec((tm, cw), lambda b, i, j: (b * nrow + i, j)),
                  pl.BlockSpec((PREP_HALO, cw),
                               lambda b, i, j: (jnp.maximum((b * nrow + i) * per_halo - 1, 0), j)),
                  pl.BlockSpec((None, PREP_HALO, cw), lambda b, i, j: (b, 0, j)),
                  pl.BlockSpec((A_CONV, cw), lambda b, i, j: (0, j))],
        out_specs=pl.BlockSpec((tm, cw), lambda b, i, j: (b * nrow + i, j)),
        out_shape=jax.ShapeDtypeStruct((B * T, A_CONV_DIM), F32),
        scratch_shapes=[pltpu.VMEM((PREP_HALO + tm, cw), F32)],
        compiler_params=_params(3),
        name="delta_prep",
    )(main, main, buf8, conv_w)


def _delta_core(qkv, z, z_col_off, beta, g, S0, norm_g, *, c=A_CHUNK):
    B, Tp, VH = beta.shape
    hd = S0.shape[-1]
    KH = (qkv.shape[-1] // hd - VH) // 2
    assert Tp % c == 0 and VH == 2 * KH and VH % (DELTA_PACK * DELTA_GROUPS) == 0
    nchunk = Tp // c
    npair = VH // (DELTA_PACK * DELTA_GROUPS)
    ngrp = VH // DELTA_PACK
    gc = jnp.cumsum(g.reshape(B, nchunk, c, VH), axis=2)
    gcol = jnp.concatenate([beta.reshape(B, nchunk, c, ngrp, DELTA_PACK),
                            gc.reshape(B, nchunk, c, ngrp, DELTA_PACK),
                            jnp.zeros((B, nchunk, c, ngrp, LANE - 2 * DELTA_PACK), F32)], axis=-1)
    gcol = gcol.reshape(B * Tp, ngrp * LANE)
    grow = jnp.transpose(gc, (0, 1, 3, 2)).reshape(B, nchunk, ngrp, DELTA_PACK, c)
    grow = jnp.pad(grow, ((0, 0), (0, 0), (0, 0), (0, 8 - DELTA_PACK), (0, 0)))
    M = B * Tp
    kw = DELTA_GROUPS * DELTA_PACK // 2 * hd
    vw = DELTA_GROUPS * DELTA_PACK * hd
    row = lambda b, hp, n: b * nchunk + n
    return pl.pallas_call(
        functools.partial(_k_delta, c=c, hd=hd),
        grid=(B, npair, nchunk),
        in_specs=[pl.BlockSpec((c, kw), lambda b, hp, n: (row(b, hp, n), hp)),
                  pl.BlockSpec((c, kw), lambda b, hp, n: (row(b, hp, n), hp + KH * hd // kw)),
                  pl.BlockSpec((c, vw), lambda b, hp, n: (row(b, hp, n), hp + 2 * KH * hd // vw)),
                  pl.BlockSpec((c, vw), lambda b, hp, n: (row(b, hp, n), hp + z_col_off)),
                  pl.BlockSpec((c, DELTA_GROUPS * LANE), lambda b, hp, n: (row(b, hp, n), hp)),
                  pl.BlockSpec((None, None, DELTA_GROUPS, 8, c), lambda b, hp, n: (b, n, hp, 0, 0)),
                  pl.BlockSpec((None, DELTA_GROUPS * DELTA_PACK, hd, hd), lambda b, hp, n: (b, hp, 0, 0)),
                  pl.BlockSpec((1, hd), lambda b, hp, n: (0, 0))],
        out_specs=[pl.BlockSpec((c, vw), lambda b, hp, n: (row(b, hp, n), hp)),
                   pl.BlockSpec((None, DELTA_GROUPS * DELTA_PACK, hd, hd), lambda b, hp, n: (b, hp, 0, 0))],
        out_shape=[jax.ShapeDtypeStruct((M, VH * hd), BF16),
                   jax.ShapeDtypeStruct(S0.shape, F32)],
        scratch_shapes=[pltpu.VMEM((DELTA_GROUPS, DELTA_PACK * c, DELTA_PACK * c), F32)],
        compiler_params=_params(3),
        name="delta_core",
    )(qkv, qkv, qkv, z, gcol, grow, S0, norm_g.reshape(1, hd))


def _rmsnorm(x, g):
    xf = x.astype(F32)
    y = xf * lax.rsqrt(jnp.mean(xf * xf, axis=-1, keepdims=True) + EPS)
    return (y * g.astype(F32)).astype(x.dtype)


def _mixer_a_core(main, small, conv_buf, S0, conv_w, a_log, dt_bias, norm_g, B, T):
    assert T >= A_CONV - 1
    b = small[:, :A_V_HEADS].reshape(B, T, A_V_HEADS)
    a = small[:, A_V_HEADS:2 * A_V_HEADS].reshape(B, T, A_V_HEADS)
    qkv = _delta_prep(main, conv_buf, conv_w, B, T, tm=min(T, 512))
    new_buf = main.reshape(B, T, -1)[:, T - (A_CONV - 1):, :A_CONV_DIM]
    beta = jax.nn.sigmoid(b)
    g = -jnp.exp(a_log) * jax.nn.softplus(a + dt_bias)
    Tp = -(-T // A_CHUNK) * A_CHUNK
    if Tp == T:
        z, z_col_off = main, A_CONV_DIM // (DELTA_GROUPS * DELTA_PACK * A_HEAD_DIM)
    else:
        padt = lambda t: jnp.pad(t, ((0, 0), (0, Tp - T), (0, 0)))
        beta, g = padt(beta), padt(g)
        qkv = padt(qkv.reshape(B, T, A_CONV_DIM)).reshape(B * Tp, A_CONV_DIM)
        z = padt(main[:, A_CONV_DIM:].reshape(B, T, A_VAL_DIM)).reshape(B * Tp, A_VAL_DIM)
        z_col_off = 0
    o, S = _delta_core(qkv, z, z_col_off, beta, g, S0, norm_g)
    if Tp != T:
        o = o.reshape(B, Tp, A_VAL_DIM)[:, :T].reshape(B * T, A_VAL_DIM)
    return o, new_buf, S


def _attend(s, v, spec):
    m = jnp.max(s, axis=-1, keepdims=True)
    p = jnp.exp(s - m)
    l = jnp.sum(p, axis=-1)
    o = jnp.einsum(spec, p, v) / jnp.swapaxes(l, -1, -2)[..., None]
    lse = jnp.swapaxes(m[..., 0] + jnp.log(l), -1, -2)
    return o, lse


def _dilated_prompt(q, k, v, dil, nw):
    B, T, H, hd = q.shape
    L = T // dil
    nb = -(-L // nw)
    lp = nb * nw
    z = B * dil

    def by_residue(a):
        return a.reshape(B, L, dil, H, hd).transpose(0, 2, 1, 3, 4).reshape(z, L, H, hd)

    qs = jnp.pad(by_residue(q), ((0, 0), (0, lp - L), (0, 0), (0, 0))).reshape(z, nb, nw, H, hd)

    def key_blocks(a):
        ap = jnp.pad(by_residue(a), ((0, 0), (nw, lp - L), (0, 0), (0, 0)))
        return jnp.concatenate([ap[:, :lp].reshape(z, nb, nw, H, hd),
                                ap[:, nw:].reshape(z, nb, nw, H, hd)], axis=2)

    kb, vb = key_blocks(k), key_blocks(v)
    qi = jnp.arange(nw)[:, None]
    ks = jnp.arange(2 * nw)[None, :]
    dist = qi + nw - ks
    first = (jnp.arange(nb) * nw - nw)[:, None, None] + ks[None]
    mask = (dist >= 0) & (dist <= nw) & (first >= 0)
    s = jnp.einsum('znqhd,znkhd->znhqk', qs, kb) * (hd ** -0.5)
    s = jnp.where(mask[None, :, None], s, -jnp.inf)
    o, lse = _attend(s, vb, 'znhqk,znkhd->znqhd')
    o = o.reshape(z, lp, H, hd)[:, :L].reshape(B, dil, L, H, hd).transpose(0, 2, 1, 3, 4).reshape(B, T, H, hd)
    lse = lse.reshape(z, lp, H)[:, :L].reshape(B, dil, L, H).transpose(0, 2, 1, 3).reshape(B, T, H)
    return o, lse


def _dilated_sample(q, k, v, k_buf, v_buf, dil, nw):
    Tn = q.shape[1]
    lb = k_buf.shape[1]
    kc = jnp.concatenate([k_buf.astype(k.dtype), k], axis=1)
    vc = jnp.concatenate([v_buf.astype(v.dtype), v], axis=1)
    idx = lb + jnp.arange(Tn)[:, None] - dil * jnp.arange(nw + 1)[None, :]
    valid = idx >= 0
    idx = jnp.maximum(idx, 0)
    kg = jnp.take(kc, idx, axis=1)
    vg = jnp.take(vc, idx, axis=1)
    s = jnp.einsum('bqhd,bqkhd->bhqk', q, kg) * (q.shape[-1] ** -0.5)
    s = jnp.where(valid[None, None], s, -jnp.inf)
    o, lse = _attend(s, vg, 'bhqk,bqkhd->bqhd')
    return o, lse, kc[:, -lb:], vc[:, -lb:]


def _mixer_c_core(qkv, bufs, q_norm_g, k_norm_g):
    B, T, _ = qkv.shape
    qkv = qkv.reshape(B, T, C_GROUPS, 3, C_HEADS, C_HEAD_DIM)
    q = _rmsnorm(qkv[:, :, :, 0], q_norm_g[:, None, :])
    k = _rmsnorm(qkv[:, :, :, 1], k_norm_g[:, None, :])
    v = qkv[:, :, :, 2]
    outs, lses, new_bufs = [], [], []
    for gi in range(C_GROUPS):
        dil = C_DILATIONS[gi]
        nw = C_WINDOWS[gi] // dil
        qg, kg, vg = q[:, :, gi], k[:, :, gi], v[:, :, gi]
        if bufs is None:
            o, lse = _dilated_prompt(qg, kg, vg, dil, nw)
            lb = min(C_WINDOWS[gi], T)
            nk, nv = kg[:, T - lb:], vg[:, T - lb:]
        else:
            o, lse, nk, nv = _dilated_sample(qg, kg, vg, bufs[gi][0], bufs[gi][1], dil, nw)
        outs.append(o)
        lses.append(lse)
        new_bufs.append((nk, nv))
    wts = jax.nn.softmax(jnp.stack(lses), axis=0)
    o = jnp.sum(wts[..., None] * jnp.stack(outs), axis=0)
    return o.reshape(B * T, D_MODEL), new_bufs


def _mixer_c_prompt(h, B, T, w_qkv, slot, q_norm_g, k_norm_g, tm):
    W = D_MODEL
    outs, lses, new_bufs = [], [], []
    for gi in range(C_GROUPS):
        dil = C_DILATIONS[gi]
        nw = C_WINDOWS[gi] // dil
        L = T // dil
        assert L % nw == 0

        def to_res(a, w, dil=dil, L=L):
            return a.reshape(B, L, dil, w).transpose(0, 2, 1, 3).reshape(B * T, w)

        def from_res(a, w, dil=dil, L=L):
            return a.reshape(B, dil, L, w).transpose(0, 2, 1, 3).reshape(B, T, w)

        hp = h if dil == 1 else to_res(h, W)
        qkv = _linear(hp, w_qkv, (slot,), n_out=3 * W, tm=tm, tn=1024, col_off=gi * 3 * W // 1024)
        o, kn, lse = _band_attn(qkv, q_norm_g[gi], k_norm_g[gi], z=B * dil, L=L,
                                heads=C_HEADS, hd=C_HEAD_DIM, nw=nw)
        lb = min(C_WINDOWS[gi], T)
        outs.append(from_res(o, W).reshape(B, T, C_HEADS, C_HEAD_DIM))
        lses.append(from_res(lse, LANE)[..., :C_HEADS])
        nk = from_res(kn, W)[:, T - lb:].reshape(B, lb, C_HEADS, C_HEAD_DIM)
        nv = from_res(qkv[:, 2 * W:], W)[:, T - lb:].reshape(B, lb, C_HEADS, C_HEAD_DIM)
        new_bufs.append((nk, nv))
    wts = jax.nn.softmax(jnp.stack(lses), axis=0)
    o = jnp.sum(wts[..., None] * jnp.stack(outs), axis=0)
    return o.reshape(B * T, D_MODEL), new_bufs


def _route(logits, tm):
    M = logits.shape[0]
    top_v, top_i = lax.top_k(logits, TOP_K)
    gates = jax.nn.softmax(top_v, axis=-1)
    e_flat = top_i.reshape(-1)
    onehot = (e_flat[:, None] == jnp.arange(N_EXPERTS)[None, :]).astype(jnp.int32)
    counts = jnp.sum(onehot, axis=0)
    rank = jnp.sum((jnp.cumsum(onehot, axis=0) - onehot) * onehot, axis=1)
    padded = ((counts + tm - 1) // tm) * tm
    ends = jnp.cumsum(padded)
    pos = (ends - padded)[e_flat] + rank
    nt = (TOP_K * M + N_EXPERTS * (tm - 1)) // tm
    src = jnp.zeros((nt * tm,), jnp.int32).at[pos].set(jnp.arange(TOP_K * M, dtype=jnp.int32) // TOP_K)
    starts = jnp.arange(nt, dtype=jnp.int32) * tm
    tile_expert = jnp.sum((starts[:, None] >= ends[None, :]).astype(jnp.int32), axis=1)
    tile_valid = (starts < ends[-1]).astype(jnp.int32)
    last_expert = tile_expert[ends[-1] // tm - 1]
    tile_expert = jnp.where(tile_valid > 0, tile_expert, last_expert).astype(jnp.int32)
    return gates, pos.reshape(M, TOP_K), src, tile_expert, tile_valid


def _moe_ffn(requests, p, layer, *, tm, tm_down):
    sizes = [h.shape[0] for h, _ in requests]
    h = jnp.concatenate([h for h, _ in requests], axis=0)
    logits = jnp.concatenate([lg for _, lg in requests], axis=0)
    gates, pos, src, tile_expert, tile_valid = _route(logits, tm)
    xs = jnp.take(h, src, axis=0)
    act = _moe_linear(xs, p['m_w_gate_up'], layer, tile_expert, tile_valid,
                      n_out=F_EXPERT, tm=tm, tn=512, up=True)
    split = tm // tm_down
    ys = _moe_linear(act, p['m_w_down'], layer, jnp.repeat(tile_expert, split),
                     jnp.repeat(tile_valid, split), n_out=D_MODEL, tm=tm_down, tn=512, up=False)
    y = (gates[:, 0:1] * jnp.take(ys, pos[:, 0], axis=0)
         + gates[:, 1:2] * jnp.take(ys, pos[:, 1], axis=0))
    out, start = [], 0
    for m in sizes:
        out.append(y[start:start + m])
        start += m
    return out


def _run_trunks(trunks, p, *, tm_moe, tm_moe_down):
    reqs = [next(t) for t in trunks]
    for layer in range(DEPTH // 2):
        ys = _moe_ffn(reqs, p, layer, tm=tm_moe, tm_down=tm_moe_down)
        nxt = []
        for t, y in zip(trunks, ys):
            try:
                nxt.append(t.send(y))
            except StopIteration as done:
                nxt.append(done.value)
        reqs = nxt
    return reqs


def _trunk(x, mod, st, p, *, tm):
    B, T, _ = x.shape
    M = B * T
    per_row = tm > T

    def vec(v):
        if per_row:
            return jnp.repeat(v, T, axis=0).reshape(1, M, D_MODEL)
        return v.reshape(B, 1, D_MODEL)

    tm_big_k = min(tm, 512)
    x = x.reshape(M, D_MODEL)
    a_conv, a_rec, b_re, b_im = [], [], [], []
    c_k = [[] for _ in range(C_GROUPS)]
    c_v = [[] for _ in range(C_GROUPS)]
    for i in range(DEPTH):
        sh1, sc1, gt1, sh2, sc2, gt2 = (vec(m) for m in jnp.split(mod[i], N_MOD, axis=-1))
        kind, slot = i % N_MIXERS, i // N_MIXERS
        if kind == 0:
            h = _modulate(x, p['g_mix'][i], sh1, sc1, tm=tm)[0]
            main = _linear(h, p['a_w_in'], (slot,), n_out=A_CONV_DIM + A_VAL_DIM, tm=tm, tn=1024)
            w_small = jnp.pad(p['a_w_in'][slot, :, A_CONV_DIM + A_VAL_DIM:],
                              ((0, 0), (0, LANE - 2 * A_V_HEADS)))
            small = _linear(h, w_small[None], (0,), n_out=LANE, tm=tm, tn=LANE)
            if st is None:
                buf0 = jnp.zeros((B, A_CONV - 1, A_CONV_DIM), F32)
                s0 = jnp.zeros((B, A_V_HEADS, A_HEAD_DIM, A_HEAD_DIM), F32)
            else:
                buf0, s0 = st['a_conv'][slot], st['a_rec'][slot]
            o, nbuf, ns = _mixer_a_core(main, small, buf0, s0, p['a_conv_w'][slot], p['a_log'][slot],
                                        p['a_dt_bias'][slot], p['a_norm_g'][slot], B, T)
            a_conv.append(nbuf)
            a_rec.append(ns)
            x = _linear(o, p['a_w_out'], (slot,), n_out=D_MODEL, tm=tm_big_k, tn=512,
                        mode="resid", res=x, gate=gt1)
        elif kind == 1:
            h, hf = _modulate(x, p['g_mix'][i], sh1, sc1, tm=tm, extra="f32")
            if st is None:
                r0 = jnp.zeros((B, B_GROUPS, B_STATE), F32)
                m0 = jnp.zeros((B, B_GROUPS, B_STATE), F32)
            else:
                r0, m0 = st['b_re'][slot], st['b_im'][slot]
            y, nr, ni = _s5_core(hf.reshape(B, T, D_MODEL), r0, m0, p['b_lambda_re'][slot],
                                 p['b_lambda_im'][slot], p['b_log_dt'][slot], p['b_B_re'][slot],
                                 p['b_B_im'][slot], p['b_C_re'][slot], p['b_C_im'][slot], p['b_D'][slot],
                                 tc=min(T, 32))
            b_re.append(nr)
            b_im.append(ni)
            x = _linear(y, p['b_w_glu'], (slot,), n_out=D_MODEL, tm=tm, tn=512,
                        mode="glu_resid", col_off2=D_MODEL // 512, res=x, gate=gt1)
        else:
            h = _modulate(x, p['g_mix'][i], sh1, sc1, tm=tm)[0]
            if st is None:
                o, nbs = _mixer_c_prompt(h, B, T, p['c_w_qkv'], slot, p['c_q_norm_g'][slot],
                                         p['c_k_norm_g'][slot], tm)
            else:
                qkv = _linear(h, p['c_w_qkv'], (slot,), n_out=C_GROUPS * 3 * D_MODEL, tm=tm, tn=512)
                bufs = [(st['c_k'][g][slot], st['c_v'][g][slot]) for g in range(C_GROUPS)]
                o, nbs = _mixer_c_core(qkv.reshape(B, T, -1), bufs, p['c_q_norm_g'][slot],
                                       p['c_k_norm_g'][slot])
            for g in range(C_GROUPS):
                c_k[g].append(nbs[g][0])
                c_v[g].append(nbs[g][1])
            x = _linear(o.astype(BF16), p['c_w_out'], (slot,), n_out=D_MODEL, tm=tm, tn=512,
                        mode="resid", res=x, gate=gt1)
        if i % 2 == 0:
            h = _modulate(x, p['g_ffn'][i], sh2, sc2, tm=tm)[0]
            act = _linear(h, p['f_w_gate_up'], (i // 2,), n_out=F_DENSE, tm=tm, tn=512, mode="swiglu",
                          col_off2=F_DENSE // 512, out_dtype=BF16)
            x = _linear(act, p['f_w_down'], (i // 2,), n_out=D_MODEL, tm=tm_big_k, tn=512,
                        mode="resid", res=x, gate=gt2)
        else:
            router = jnp.pad(p['m_router'][i // 2], ((0, 0), (0, LANE - N_EXPERTS)))
            h, logits = _modulate(x, p['g_ffn'][i], sh2, sc2, tm=tm, extra="router", router=router)
            y = yield h, logits[:, :N_EXPERTS]
            gt2_rows = gt2.reshape(M, D_MODEL) if per_row else jnp.repeat(gt2.reshape(B, D_MODEL), T, axis=0)
            x = x + gt2_rows * y
    stk = lambda lst: jnp.stack(lst, axis=0)
    states = (stk(a_conv), stk(a_rec), stk(b_re), stk(b_im), stk(c_k[0]), stk(c_v[0]),
              stk(c_k[1]), stk(c_v[1]), stk(c_k[2]), stk(c_v[2]))
    return x.reshape(B, T, D_MODEL), states


def kernel(x_prompt, x_sample, c_prompt, c_sample, state_a_conv, state_a_rec, state_b_re, state_b_im, cache_c_k0, cache_c_v0, cache_c_k1, cache_c_v1, cache_c_k2, cache_c_v2, g_mix, g_ffn, w_mod, b_mod, a_w_in, a_conv_w, a_log, a_dt_bias, a_norm_g, a_w_out, b_lambda_re, b_lambda_im, b_log_dt, b_B_re, b_B_im, b_C_re, b_C_im, b_D, b_w_glu, c_w_qkv, c_q_norm_g, c_k_norm_g, c_w_out, f_w_gate_up, f_w_down, m_router, m_w_gate_up, m_w_down):
    p = dict(g_mix=g_mix, g_ffn=g_ffn,
             a_w_in=a_w_in, a_conv_w=a_conv_w, a_log=a_log, a_dt_bias=a_dt_bias, a_norm_g=a_norm_g, a_w_out=a_w_out,
             b_lambda_re=b_lambda_re, b_lambda_im=b_lambda_im, b_log_dt=b_log_dt, b_B_re=b_B_re, b_B_im=b_B_im,
             b_C_re=b_C_re, b_C_im=b_C_im, b_D=b_D, b_w_glu=b_w_glu,
             c_w_qkv=c_w_qkv, c_q_norm_g=c_q_norm_g, c_k_norm_g=c_k_norm_g, c_w_out=c_w_out,
             f_w_gate_up=f_w_gate_up, f_w_down=f_w_down,
             m_router=m_router, m_w_gate_up=m_w_gate_up, m_w_down=m_w_down)
    st = dict(a_conv=state_a_conv, a_rec=state_a_rec, b_re=state_b_re, b_im=state_b_im,
              c_k=(cache_c_k0, cache_c_k1, cache_c_k2), c_v=(cache_c_v0, cache_c_v1, cache_c_v2))
    nb_p, nb_s = c_prompt.shape[0], c_sample.shape[0]
    c_all = jnp.concatenate([c_prompt, c_sample, jnp.zeros((16 - nb_p - nb_s, D_MODEL), F32)], axis=0)
    mod = _adaln(c_all, w_mod, b_mod)
    trunks = [_trunk(x_prompt, mod[:, :nb_p], None, p, tm=1024),
              _trunk(x_sample, mod[:, nb_p:nb_p + nb_s], st, p, tm=64)]
    (y_prompt, ps), (y_sample, ss) = _run_trunks(trunks, p, tm_moe=512, tm_moe_down=256)
    return (y_prompt, y_sample) + ps + ss
```

```python
import functools

import jax
import jax.numpy as jnp
from jax import lax
from jax.experimental import pallas as pl
from jax.experimental.pallas import tpu as pltpu

D_MODEL = 2048
DEPTH = 4
N_MIXERS = 3
N_MOD = 6
EPS = 1e-6

A_K_HEADS = 16
A_V_HEADS = 32
A_HEAD_DIM = 128
A_KEY_DIM = A_K_HEADS * A_HEAD_DIM
A_VAL_DIM = A_V_HEADS * A_HEAD_DIM
A_CONV_DIM = 2 * A_KEY_DIM + A_VAL_DIM
A_CONV = 4
A_CHUNK = 64

B_GROUP = 16
B_GROUPS = D_MODEL // B_GROUP
B_STATE = 64

C_HEADS = 16
C_HEAD_DIM = D_MODEL // C_HEADS
C_WINDOWS = (128, 512, 2048)
C_DILATIONS = (1, 4, 16)
C_GROUPS = 3

F_DENSE = 5632
N_EXPERTS = 8
TOP_K = 2
F_EXPERT = 7168

LANE = 128
VMEM_LIMIT = 56 * 1024 * 1024
BF16 = jnp.bfloat16
F32 = jnp.float32


def _params(n_grid):
    return pltpu.CompilerParams(dimension_semantics=("arbitrary",) * n_grid,
                                vmem_limit_bytes=VMEM_LIMIT)


def _cast_weights_on_first_row_tile(w_refs, wb_refs):
    @pl.when(pl.program_id(1) == 0)
    def _():
        for w_ref, wb_ref in zip(w_refs, wb_refs):
            wb_ref[...] = w_ref[...].astype(BF16)


def _dot(x, wb_ref):
    return jnp.dot(x, wb_ref[...], preferred_element_type=F32)


def _k_plain(x_ref, w_ref, o_ref, wb_ref):
    _cast_weights_on_first_row_tile((w_ref,), (wb_ref,))
    o_ref[...] = _dot(x_ref[...], wb_ref).astype(o_ref.dtype)


def _k_resid(x_ref, w_ref, r_ref, g_ref, o_ref, wb_ref):
    _cast_weights_on_first_row_tile((w_ref,), (wb_ref,))
    o_ref[...] = r_ref[...] + g_ref[...] * _dot(x_ref[...], wb_ref)


def _k_swiglu(x_ref, wg_ref, wu_ref, o_ref, wgb_ref, wub_ref):
    _cast_weights_on_first_row_tile((wg_ref, wu_ref), (wgb_ref, wub_ref))
    x = x_ref[...]
    g = _dot(x, wgb_ref)
    u = _dot(x, wub_ref)
    o_ref[...] = (g * jax.nn.sigmoid(g) * u).astype(o_ref.dtype)


def _k_glu_resid(x_ref, wa_ref, wb_ref, r_ref, g_ref, o_ref, wab_ref, wbb_ref):
    _cast_weights_on_first_row_tile((wa_ref, wb_ref), (wab_ref, wbb_ref))
    x = x_ref[...]
    ya = _dot(x, wab_ref)
    yb = _dot(x, wbb_ref)
    o_ref[...] = r_ref[...] + g_ref[...] * (ya * jax.nn.sigmoid(yb))


def _linear(x, w, widx, *, n_out, tm, tn, mode="plain", col_off=0, col_off2=0,
            res=None, gate=None, out_dtype=F32):
    M, K = x.shape
    assert M % tm == 0 and n_out % tn == 0
    ni, nj = M // tm, n_out // tn
    lead = (None,) * len(widx)

    def wspec(off):
        return pl.BlockSpec(lead + (K, tn), lambda j, i: tuple(widx) + (0, j + off))

    xspec = pl.BlockSpec((tm, K), lambda j, i: (i, 0))
    ospec = pl.BlockSpec((tm, tn), lambda j, i: (i, j))
    two = mode in ("swiglu", "glu_resid")
    in_specs = [xspec, wspec(col_off)] + ([wspec(col_off2)] if two else [])
    args = [x, w] + ([w] if two else [])
    if mode in ("resid", "glu_resid"):
        tiles_per_gate = ni // gate.shape[0]
        in_specs += [ospec,
                     pl.BlockSpec((None, gate.shape[1], tn),
                                  lambda j, i: (i // tiles_per_gate, 0, j))]
        args += [res, gate]
    body = {"plain": _k_plain, "resid": _k_resid, "swiglu": _k_swiglu,
            "glu_resid": _k_glu_resid}[mode]
    return pl.pallas_call(
        body,
        grid=(nj, ni),
        in_specs=in_specs,
        out_specs=ospec,
        out_shape=jax.ShapeDtypeStruct((M, n_out), out_dtype),
        scratch_shapes=[pltpu.VMEM((K, tn), BF16)] * (2 if two else 1),
        compiler_params=_params(2),
        name="linear_" + mode,
    )(*args)


def _cast_weights_on_expert_change(te_ref, w_refs, wb_refs):
    t = pl.program_id(1)
    prev = te_ref[jnp.maximum(t - 1, 0)]

    @pl.when((t == 0) | (te_ref[t] != prev))
    def _():
        for w_ref, wb_ref in zip(w_refs, wb_refs):
            wb_ref[...] = w_ref[...].astype(BF16)


def _k_moe_up(te_ref, tv_ref, x_ref, wg_ref, wu_ref, o_ref, wgb_ref, wub_ref):
    _cast_weights_on_expert_change(te_ref, (wg_ref, wu_ref), (wgb_ref, wub_ref))
    valid = tv_ref[pl.program_id(1)] > 0

    @pl.when(valid)
    def _():
        x = x_ref[...]
        g = _dot(x, wgb_ref)
        u = _dot(x, wub_ref)
        o_ref[...] = (g * jax.nn.sigmoid(g) * u).astype(o_ref.dtype)

    @pl.when(jnp.logical_not(valid))
    def _():
        o_ref[...] = jnp.zeros_like(o_ref)


def _k_moe_down(te_ref, tv_ref, x_ref, w_ref, o_ref, wb_ref):
    _cast_weights_on_expert_change(te_ref, (w_ref,), (wb_ref,))
    valid = tv_ref[pl.program_id(1)] > 0

    @pl.when(valid)
    def _():
        o_ref[...] = _dot(x_ref[...], wb_ref)

    @pl.when(jnp.logical_not(valid))
    def _():
        o_ref[...] = jnp.zeros_like(o_ref)


def _moe_linear(x, w, layer, tile_expert, tile_valid, *, n_out, tm, tn, up):
    Mp, K = x.shape
    nt, nj = Mp // tm, n_out // tn

    def wspec(off):
        return pl.BlockSpec((None, None, K, tn),
                            lambda j, t, te, tv: (layer, te[t], 0, j + off))

    xspec = pl.BlockSpec((tm, K), lambda j, t, te, tv: (t, 0))
    ospec = pl.BlockSpec((tm, tn), lambda j, t, te, tv: (t, j))
    in_specs = [xspec, wspec(0)] + ([wspec(nj)] if up else [])
    args = [x, w] + ([w] if up else [])
    return pl.pallas_call(
        _k_moe_up if up else _k_moe_down,
        grid_spec=pltpu.PrefetchScalarGridSpec(
            num_scalar_prefetch=2,
            grid=(nj, nt),
            in_specs=in_specs,
            out_specs=ospec,
            scratch_shapes=[pltpu.VMEM((K, tn), BF16)] * (2 if up else 1)),
        out_shape=jax.ShapeDtypeStruct((Mp, n_out), BF16 if up else F32),
        compiler_params=_params(2),
        name="moe_up" if up else "moe_down",
    )(tile_expert, tile_valid, *args)


def _k_adaln(c_ref, w_ref, b_ref, o_ref):
    c = c_ref[...]
    a = (c * jax.nn.sigmoid(c)).astype(BF16)
    o_ref[...] = jnp.dot(a, w_ref[...].astype(BF16), preferred_element_type=F32) + b_ref[...]


def _adaln(c_all, w_mod, b_mod):
    R = c_all.shape[0]
    tn = 1024
    n = N_MOD * D_MODEL
    return pl.pallas_call(
        _k_adaln,
        grid=(DEPTH, n // tn),
        in_specs=[pl.BlockSpec((R, D_MODEL), lambda l, j: (0, 0)),
                  pl.BlockSpec((None, D_MODEL, tn), lambda l, j: (l, 0, j)),
                  pl.BlockSpec((None, 1, tn), lambda l, j: (l, 0, j))],
        out_specs=pl.BlockSpec((None, R, tn), lambda l, j: (l, 0, j)),
        out_shape=jax.ShapeDtypeStruct((DEPTH, R, n), F32),
        compiler_params=_params(2),
        name="adaln",
    )(c_all, w_mod, b_mod.reshape(DEPTH, 1, n))


def _modulated(x_ref, g_ref, sh_ref, sc_ref):
    x = x_ref[...]
    y = x * lax.rsqrt(jnp.mean(x * x, axis=-1, keepdims=True) + EPS)
    return (y * g_ref[...]) * (1.0 + sc_ref[...]) + sh_ref[...]


def _k_modulate(x_ref, g_ref, sh_ref, sc_ref, hb_ref):
    hb_ref[...] = _modulated(x_ref, g_ref, sh_ref, sc_ref).astype(BF16)


def _k_modulate_f32(x_ref, g_ref, sh_ref, sc_ref, hb_ref, hf_ref):
    h = _modulated(x_ref, g_ref, sh_ref, sc_ref)
    hb_ref[...] = h.astype(BF16)
    hf_ref[...] = h


def _k_modulate_router(x_ref, g_ref, sh_ref, sc_ref, r_ref, hb_ref, lg_ref):
    h = _modulated(x_ref, g_ref, sh_ref, sc_ref)
    hb = h.astype(BF16)
    hb_ref[...] = hb
    lg_ref[...] = jnp.dot(hb, r_ref[...].astype(BF16), preferred_element_type=F32)


def _modulate(x, g, shift, scale, *, tm, extra=None, router=None):
    M = x.shape[0]
    ni = M // tm
    tiles_per_vec = ni // shift.shape[0]
    xspec = pl.BlockSpec((tm, D_MODEL), lambda i: (i, 0))
    vspec = pl.BlockSpec((None, shift.shape[1], D_MODEL), lambda i: (i // tiles_per_vec, 0, 0))
    in_specs = [xspec, pl.BlockSpec((1, D_MODEL), lambda i: (0, 0)), vspec, vspec]
    args = [x, g.reshape(1, D_MODEL), shift, scale]
    out_specs = [xspec]
    out_shape = [jax.ShapeDtypeStruct((M, D_MODEL), BF16)]
    body = _k_modulate
    if extra == "f32":
        body = _k_modulate_f32
        out_specs.append(xspec)
        out_shape.append(jax.ShapeDtypeStruct((M, D_MODEL), F32))
    elif extra == "router":
        body = _k_modulate_router
        in_specs.append(pl.BlockSpec((D_MODEL, LANE), lambda i: (0, 0)))
        args.append(router)
        out_specs.append(pl.BlockSpec((tm, LANE), lambda i: (i, 0)))
        out_shape.append(jax.ShapeDtypeStruct((M, LANE), F32))
    return pl.pallas_call(
        body, grid=(ni,), in_specs=in_specs, out_specs=out_specs, out_shape=out_shape,
        compiler_params=_params(1), name="modulate",
    )(*args)


S5_SUPER = 16
S5_ROWS = 8


def _k_s5(u_ref, bb_ref, cre_ref, cim_ref, are_ref, aim_ref, d_ref, s0re_ref, s0im_ref,
          y_ref, fre_ref, fim_ref, xre_ref, xim_ref, car_re, car_im, *, tc):
    @pl.when(pl.program_id(0) == 0)
    def _():
        car_re[...] = s0re_ref[...]
        car_im[...] = s0im_ref[...]

    ub = u_ref[...].astype(BF16)
    for s in range(S5_SUPER):
        bu = jnp.dot(ub[:, s * 128:(s + 1) * 128], bb_ref[s], preferred_element_type=F32)
        xre_ref[:, s * 512:(s + 1) * 512] = bu[:, :512]
        xim_ref[:, s * 512:(s + 1) * 512] = bu[:, 512:]

    for s in range(S5_SUPER):
        cols = slice(s * 512, (s + 1) * 512)
        a_re = jnp.broadcast_to(are_ref[:, cols], (S5_ROWS, 512))
        a_im = jnp.broadcast_to(aim_ref[:, cols], (S5_ROWS, 512))

        def body(t, carry, cols=cols, a_re=a_re, a_im=a_im):
            xr, xi = carry
            r0 = pl.multiple_of(t * S5_ROWS, S5_ROWS)
            nr = (a_re * xr - a_im * xi) + xre_ref[pl.ds(r0, S5_ROWS), cols]
            ni = (a_re * xi + a_im * xr) + xim_ref[pl.ds(r0, S5_ROWS), cols]
            xre_ref[pl.ds(r0, S5_ROWS), cols] = nr
            xim_ref[pl.ds(r0, S5_ROWS), cols] = ni
            return nr, ni

        xr, xi = lax.fori_loop(0, tc, body, (car_re[:, cols], car_im[:, cols]))
        car_re[:, cols] = xr
        car_im[:, cols] = xi

    for s in range(S5_SUPER):
        cols = slice(s * 512, (s + 1) * 512)
        lanes = slice(s * 128, (s + 1) * 128)
        y = (jnp.dot(xre_ref[:, cols].astype(BF16), cre_ref[s], preferred_element_type=F32)
             - jnp.dot(xim_ref[:, cols].astype(BF16), cim_ref[s], preferred_element_type=F32))
        y = y + d_ref[:, lanes] * u_ref[:, lanes]
        y_ref[:, lanes] = jax.nn.gelu(y).astype(BF16)
    fre_ref[...] = car_re[...]
    fim_ref[...] = car_im[...]


def _s5_core(u, s_re, s_im, lam_re, lam_im, log_dt, b_re, b_im, c_re, c_im, d_skip, *, tc):
    B, T, D = u.shape
    G, P = lam_re.shape
    Cg = D // G
    assert G == S5_SUPER * 8 and 8 * Cg == LANE and B <= S5_ROWS and T % tc == 0
    dt = jnp.exp(log_dt)[:, None]
    mag = jnp.exp(lam_re * dt)
    ab_re, ab_im = mag * jnp.cos(lam_im * dt), mag * jnp.sin(lam_im * dt)
    den = lam_re * lam_re + lam_im * lam_im
    f_re = ((ab_re - 1.0) * lam_re + ab_im * lam_im) / den
    f_im = (ab_im * lam_re - (ab_re - 1.0) * lam_im) / den
    bb_re = f_re[..., None] * b_re - f_im[..., None] * b_im
    bb_im = f_re[..., None] * b_im + f_im[..., None] * b_re
    eye = jnp.eye(8, dtype=F32)

    def blockdiag_in(bb):
        t = bb.reshape(S5_SUPER, 8, P, Cg)
        return jnp.einsum('sgpc,gh->sgchp', t, eye).reshape(S5_SUPER, 8 * Cg, 8 * P)

    def blockdiag_out(c):
        t = c.reshape(S5_SUPER, 8, Cg, P)
        return jnp.einsum('sgcp,gh->sgphc', t, eye).reshape(S5_SUPER, 8 * P, 8 * Cg)

    bb = jnp.concatenate([blockdiag_in(bb_re), blockdiag_in(bb_im)], axis=-1).astype(BF16)
    cre = blockdiag_out(c_re).astype(BF16)
    cim = blockdiag_out(c_im).astype(BF16)
    pad = S5_ROWS - B
    u_tb = jnp.pad(jnp.transpose(u, (1, 0, 2)), ((0, 0), (0, pad), (0, 0))).reshape(T * S5_ROWS, D)
    s0re = jnp.pad(s_re.reshape(B, G * P), ((0, pad), (0, 0)))
    s0im = jnp.pad(s_im.reshape(B, G * P), ((0, pad), (0, 0)))
    R = tc * S5_ROWS
    NS = G * P
    full = lambda shp: pl.BlockSpec(shp, lambda i: (0,) * len(shp))
    y, fre, fim = pl.pallas_call(
        functools.partial(_k_s5, tc=tc),
        grid=(T // tc,),
        in_specs=[pl.BlockSpec((R, D), lambda i: (i, 0)),
                  full((S5_SUPER, 8 * Cg, 16 * P)), full((S5_SUPER, 8 * P, 8 * Cg)),
                  full((S5_SUPER, 8 * P, 8 * Cg)),
                  full((1, NS)), full((1, NS)), full((1, D)), full((S5_ROWS, NS)), full((S5_ROWS, NS))],
        out_specs=[pl.BlockSpec((R, D), lambda i: (i, 0)), full((S5_ROWS, NS)), full((S5_ROWS, NS))],
        out_shape=[jax.ShapeDtypeStruct((T * S5_ROWS, D), BF16),
                   jax.ShapeDtypeStruct((S5_ROWS, NS), F32), jax.ShapeDtypeStruct((S5_ROWS, NS), F32)],
        scratch_shapes=[pltpu.VMEM((R, NS), F32), pltpu.VMEM((R, NS), F32),
                        pltpu.VMEM((S5_ROWS, NS), F32), pltpu.VMEM((S5_ROWS, NS), F32)],
        compiler_params=_params(1),
        name="s5_core",
    )(u_tb, bb, cre, cim, ab_re.reshape(1, NS), ab_im.reshape(1, NS), d_skip.reshape(1, D), s0re, s0im)
    y = jnp.transpose(y.reshape(T, S5_ROWS, D)[:, :B], (1, 0, 2)).reshape(B * T, D)
    return y, fre[:B].reshape(B, G, P), fim[:B].reshape(B, G, P)


def _k_band_attn(q_ref, kc_ref, kp_ref, vc_ref, vp_ref, qg_ref, kg_ref, o_ref, kn_ref, lse_ref,
                 *, heads, hd, nw):
    bi = pl.program_id(1)
    qi = lax.broadcasted_iota(jnp.int32, (nw, nw), 0)
    kj = lax.broadcasted_iota(jnp.int32, (nw, nw), 1)
    cur_ok = kj <= qi
    prev_ok = jnp.logical_and(kj >= qi, bi > 0)
    lane = lax.broadcasted_iota(jnp.int32, (nw, LANE), 1)
    scale = hd ** -0.5
    qg = qg_ref[...]
    kg = kg_ref[...]
    lse_tile = jnp.zeros((nw, LANE), F32)

    def rms(x, g):
        return (x * lax.rsqrt(jnp.mean(x * x, axis=-1, keepdims=True) + EPS)) * g

    nt = (((1,), (1,)), ((), ()))
    for h in range(heads):
        sl = slice(h * hd, (h + 1) * hd)
        q = rms(q_ref[:, sl], qg).astype(BF16)
        kc = rms(kc_ref[:, sl], kg)
        kn_ref[:, sl] = kc
        kp = rms(kp_ref[:, sl], kg).astype(BF16)
        s_c = lax.dot_general(q, kc.astype(BF16), nt, preferred_element_type=F32) * scale
        s_p = lax.dot_general(q, kp, nt, preferred_element_type=F32) * scale
        s_c = jnp.where(cur_ok, s_c, -jnp.inf)
        s_p = jnp.where(prev_ok, s_p, -jnp.inf)
        m = jnp.maximum(jnp.max(s_c, axis=-1, keepdims=True), jnp.max(s_p, axis=-1, keepdims=True))
        p_c = jnp.exp(s_c - m)
        p_p = jnp.exp(s_p - m)
        l = jnp.sum(p_c, axis=-1, keepdims=True) + jnp.sum(p_p, axis=-1, keepdims=True)
        o = (jnp.dot(p_p.astype(BF16), vp_ref[:, sl].astype(BF16), preferred_element_type=F32)
             + jnp.dot(p_c.astype(BF16), vc_ref[:, sl].astype(BF16), preferred_element_type=F32))
        o_ref[:, sl] = o / l
        lse_tile = jnp.where(lane == h, m + jnp.log(l), lse_tile)
    lse_ref[...] = lse_tile


def _band_attn(qkv, q_g, k_g, *, z, L, heads, hd, nw):
    W = heads * hd
    nb = L // nw
    x = qkv.reshape(z, L, 3 * W)

    def blk(col, prev):
        if prev:
            return pl.BlockSpec((None, nw, W), lambda zi, bi: (zi, jnp.maximum(bi - 1, 0), col))
        return pl.BlockSpec((None, nw, W), lambda zi, bi: (zi, bi, col))

    gspec = pl.BlockSpec((1, hd), lambda zi, bi: (0, 0))
    ospec = pl.BlockSpec((None, nw, W), lambda zi, bi: (zi, bi, 0))
    lspec = pl.BlockSpec((None, nw, LANE), lambda zi, bi: (zi, bi, 0))
    o, kn, lse = pl.pallas_call(
        functools.partial(_k_band_attn, heads=heads, hd=hd, nw=nw),
        grid=(z, nb),
        in_specs=[blk(0, False), blk(1, False), blk(1, True), blk(2, False), blk(2, True), gspec, gspec],
        out_specs=[ospec, ospec, lspec],
        out_shape=[jax.ShapeDtypeStruct((z, L, W), F32), jax.ShapeDtypeStruct((z, L, W), F32),
                   jax.ShapeDtypeStruct((z, L, LANE), F32)],
        compiler_params=_params(2),
        name="band_attn",
    )(x, x, x, x, x, q_g.reshape(1, hd), k_g.reshape(1, hd))
    return o.reshape(z * L, W), kn.reshape(z * L, W), lse.reshape(z * L, LANE)


DELTA_PACK = 4
DELTA_GROUPS = 4


def _split_bf16(x):
    hi = x.astype(BF16)
    return hi, (x - hi.astype(F32)).astype(BF16)


def _dot3(a, b):
    d = lambda x, y: jnp.dot(x, y, preferred_element_type=F32)
    return d(a[0], b[0]) + (d(a[0], b[1]) + d(a[1], b[0]))


def _k_delta(q_ref, k_ref, v_ref, z_ref, gcol_ref, grow_ref, s0_ref, ng_ref, o_ref, s_ref, ablk_ref,
             *, c, hd):
    @pl.when(pl.program_id(2) == 0)
    def _():
        s_ref[...] = s0_ref[...]
        ablk_ref[...] = jnp.zeros_like(ablk_ref)

    ri = lax.broadcasted_iota(jnp.int32, (c, c), 0)
    ci = lax.broadcasted_iota(jnp.int32, (c, c), 1)
    causal = ci <= ri
    strict = ci < ri
    nt = (((1,), (1,)), ((), ()))
    tn = (((0,), (0,)), ((), ()))
    ng = ng_ref[...]
    heads, a_all, x_all = [], [], []
    for g in range(DELTA_GROUPS):
        rhs_rows = []
        for i in range(DELTA_PACK):
            kh = (g * DELTA_PACK + i) // 2
            col = g * LANE + i
            beta = gcol_ref[:, col:col + 1]
            gc = gcol_ref[:, col + DELTA_PACK:col + DELTA_PACK + 1]
            gr = grow_ref[g, i:i + 1, :]
            q = q_ref[:, kh * hd:(kh + 1) * hd]
            k = k_ref[:, kh * hd:(kh + 1) * hd]
            v = v_ref[:, (g * DELTA_PACK + i) * hd:(g * DELTA_PACK + i + 1) * hd]
            decay = jnp.where(causal, jnp.exp(jnp.where(causal, gc - gr, 0.0)), 0.0)
            kb = k * beta
            kbf = k.astype(BF16)
            kk = lax.dot_general(kb.astype(BF16), kbf, nt, preferred_element_type=F32)
            ablk_ref[g, i * c:(i + 1) * c, i * c:(i + 1) * c] = jnp.where(strict, kk * decay, 0.0)
            qk = lax.dot_general(q.astype(BF16), kbf, nt, preferred_element_type=F32) * decay
            egc = jnp.exp(gc)
            rhs_rows.append(jnp.concatenate([v * beta, kb * egc], axis=1))
            heads.append((q, k, gc, egc, qk))
        a = _split_bf16(ablk_ref[g])
        r = jnp.concatenate(rhs_rows, axis=0)
        a_all.append(a)
        x_all.append(r - _dot3(a, _split_bf16(r)))
    p_all = a_all
    m = 1
    while 2 * m < c:
        p_all = [_split_bf16(_dot3(p, p)) for p in p_all]
        x_all = [x + _dot3(p, _split_bf16(x)) for p, x in zip(p_all, x_all)]
        m *= 2
    for g in range(DELTA_GROUPS):
        x = x_all[g]
        for i in range(DELTA_PACK):
            hidx = g * DELTA_PACK + i
            q, k, gc, egc, qk = heads[hidx]
            u = x[i * c:(i + 1) * c, :hd]
            w = x[i * c:(i + 1) * c, hd:]
            s = s_ref[hidx]
            sb = s.astype(BF16)
            v_new = u - jnp.dot(w.astype(BF16), sb, preferred_element_type=F32)
            vnb = v_new.astype(BF16)
            o = (jnp.dot((q * egc).astype(BF16), sb, preferred_element_type=F32)
                 + jnp.dot(qk.astype(BF16), vnb, preferred_element_type=F32))
            g_last = gc[c - 1:c, :]
            kd = (k * jnp.exp(g_last - gc)).astype(BF16)
            s_ref[hidx] = s * jnp.exp(g_last) + lax.dot_general(kd, vnb, tn, preferred_element_type=F32)
            on = (o * lax.rsqrt(jnp.mean(o * o, axis=-1, keepdims=True) + EPS)) * ng
            zz = z_ref[:, hidx * hd:(hidx + 1) * hd]
            o_ref[:, hidx * hd:(hidx + 1) * hd] = (on * (zz * jax.nn.sigmoid(zz))).astype(BF16)


def _k_delta_prep(cur_ref, prev_ref, buf_ref, w_ref, o_ref, up_ref, *, tm, hd, n_q_tiles):
    halo = up_ref.shape[0] - tm
    first = pl.program_id(1) == 0
    up_ref[0:halo, :] = jnp.where(first, buf_ref[...], prev_ref[...])
    up_ref[halo:halo + tm, :] = cur_ref[...]
    start = halo - (A_CONV - 1)
    y = up_ref[start:start + tm, :] * w_ref[0:1, :]
    for j in range(1, A_CONV):
        y = y + up_ref[start + j:start + j + tm, :] * w_ref[j:j + 1, :]
    y = y * jax.nn.sigmoid(y)
    col = pl.program_id(2)
    scale = jnp.where(col < n_q_tiles, hd ** -0.5, 1.0)
    is_qk = col < 2 * n_q_tiles
    for h in range(cur_ref.shape[1] // hd):
        yh = y[:, h * hd:(h + 1) * hd]
        nh = (yh * lax.rsqrt(jnp.sum(yh * yh, axis=-1, keepdims=True) + EPS)) * scale
        o_ref[:, h * hd:(h + 1) * hd] = jnp.where(is_qk, nh, yh)


PREP_HALO = 8


def _delta_prep(main, conv_buf, conv_w, B, T, *, tm, cw=512):
    assert T % tm == 0 and tm % PREP_HALO == 0 and A_KEY_DIM % cw == 0
    buf8 = jnp.pad(conv_buf, ((0, 0), (PREP_HALO - (A_CONV - 1), 0), (0, 0)))
    nrow = T // tm
    per_halo = tm // PREP_HALO
    return pl.pallas_call(
        functools.partial(_k_delta_prep, tm=tm, hd=A_HEAD_DIM, n_q_tiles=A_KEY_DIM // cw),
        grid=(B, nrow, A_CONV_DIM // cw),
        in_specs=[pl.BlockSpec((tm, cw), lambda b, i, j: (b * nrow + i, j)),
                  pl.BlockSpec((PREP_HALO, cw),
                               lambda b, i, j: (jnp.maximum((b * nrow + i) * per_halo - 1, 0), j)),
                  pl.BlockSpec((None, PREP_HALO, cw), lambda b, i, j: (b, 0, j)),
                  pl.BlockSpec((A_CONV, cw), lambda b, i, j: (0, j))],
        out_specs=pl.BlockSpec((tm, cw), lambda b, i, j: (b * nrow + i, j)),
        out_shape=jax.ShapeDtypeStruct((B * T, A_CONV_DIM), F32),
        scratch_shapes=[pltpu.VMEM((PREP_HALO + tm, cw), F32)],
        compiler_params=_params(3),
        name="delta_prep",
    )(main, main, buf8, conv_w)


def _delta_core(qkv, z, z_col_off, beta, g, S0, norm_g, *, c=A_CHUNK):
    B, Tp, VH = beta.shape
    hd = S0.shape[-1]
    KH = (qkv.shape[-1] // hd - VH) // 2
    assert Tp % c == 0 and VH == 2 * KH and VH % (DELTA_PACK * DELTA_GROUPS) == 0
    nchunk = Tp // c
    npair = VH // (DELTA_PACK * DELTA_GROUPS)
    ngrp = VH // DELTA_PACK
    gc = jnp.cumsum(g.reshape(B, nchunk, c, VH), axis=2)
    gcol = jnp.concatenate([beta.reshape(B, nchunk, c, ngrp, DELTA_PACK),
                            gc.reshape(B, nchunk, c, ngrp, DELTA_PACK),
                            jnp.zeros((B, nchunk, c, ngrp, LANE - 2 * DELTA_PACK), F32)], axis=-1)
    gcol = gcol.reshape(B * Tp, ngrp * LANE)
    grow = jnp.transpose(gc, (0, 1, 3, 2)).reshape(B, nchunk, ngrp, DELTA_PACK, c)
    grow = jnp.pad(grow, ((0, 0), (0, 0), (0, 0), (0, 8 - DELTA_PACK), (0, 0)))
    M = B * Tp
    kw = DELTA_GROUPS * DELTA_PACK // 2 * hd
    vw = DELTA_GROUPS * DELTA_PACK * hd
    row = lambda b, hp, n: b * nchunk + n
    return pl.pallas_call(
        functools.partial(_k_delta, c=c, hd=hd),
        grid=(B, npair, nchunk),
        in_specs=[pl.BlockSpec((c, kw), lambda b, hp, n: (row(b, hp, n), hp)),
                  pl.BlockSpec((c, kw), lambda b, hp, n: (row(b, hp, n), hp + KH * hd // kw)),
                  pl.BlockSpec((c, vw), lambda b, hp, n: (row(b, hp, n), hp + 2 * KH * hd // vw)),
                  pl.BlockSpec((c, vw), lambda b, hp, n: (row(b, hp, n), hp + z_col_off)),
                  pl.BlockSpec((c, DELTA_GROUPS * LANE), lambda b, hp, n: (row(b, hp, n), hp)),
                  pl.BlockSpec((None, None, DELTA_GROUPS, 8, c), lambda b, hp, n: (b, n, hp, 0, 0)),
                  pl.BlockSpec((None, DELTA_GROUPS * DELTA_PACK, hd, hd), lambda b, hp, n: (b, hp, 0, 0)),
                  pl.BlockSpec((1, hd), lambda b, hp, n: (0, 0))],
        out_specs=[pl.BlockSpec((c, vw), lambda b, hp, n: (row(b, hp, n), hp)),
                   pl.BlockSpec((None, DELTA_GROUPS * DELTA_PACK, hd, hd), lambda b, hp, n: (b, hp, 0, 0))],
        out_shape=[jax.ShapeDtypeStruct((M, VH * hd), BF16),
                   jax.ShapeDtypeStruct(S0.shape, F32)],
        scratch_shapes=[pltpu.VMEM((DELTA_GROUPS, DELTA_PACK * c, DELTA_PACK * c), F32)],
        compiler_params=_params(3),
        name="delta_core",
    )(qkv, qkv, qkv, z, gcol, grow, S0, norm_g.reshape(1, hd))


def _rmsnorm(x, g):
    xf = x.astype(F32)
    y = xf * lax.rsqrt(jnp.mean(xf * xf, axis=-1, keepdims=True) + EPS)
    return (y * g.astype(F32)).astype(x.dtype)


def _mixer_a_core(main, small, conv_buf, S0, conv_w, a_log, dt_bias, norm_g, B, T):
    assert T >= A_CONV - 1
    b = small[:, :A_V_HEADS].reshape(B, T, A_V_HEADS)
    a = small[:, A_V_HEADS:2 * A_V_HEADS].reshape(B, T, A_V_HEADS)
    qkv = _delta_prep(main, conv_buf, conv_w, B, T, tm=min(T, 512))
    new_buf = main.reshape(B, T, -1)[:, T - (A_CONV - 1):, :A_CONV_DIM]
    beta = jax.nn.sigmoid(b)
    g = -jnp.exp(a_log) * jax.nn.softplus(a + dt_bias)
    Tp = -(-T // A_CHUNK) * A_CHUNK
    if Tp == T:
        z, z_col_off = main, A_CONV_DIM // (DELTA_GROUPS * DELTA_PACK * A_HEAD_DIM)
    else:
        padt = lambda t: jnp.pad(t, ((0, 0), (0, Tp - T), (0, 0)))
        beta, g = padt(beta), padt(g)
        qkv = padt(qkv.reshape(B, T, A_CONV_DIM)).reshape(B * Tp, A_CONV_DIM)
        z = padt(main[:, A_CONV_DIM:].reshape(B, T, A_VAL_DIM)).reshape(B * Tp, A_VAL_DIM)
        z_col_off = 0
    o, S = _delta_core(qkv, z, z_col_off, beta, g, S0, norm_g)
    if Tp != T:
        o = o.reshape(B, Tp, A_VAL_DIM)[:, :T].reshape(B * T, A_VAL_DIM)
    return o, new_buf, S


def _attend(s, v, spec):
    m = jnp.max(s, axis=-1, keepdims=True)
    p = jnp.exp(s - m)
    l = jnp.sum(p, axis=-1)
    o = jnp.einsum(spec, p, v) / jnp.swapaxes(l, -1, -2)[..., None]
    lse = jnp.swapaxes(m[..., 0] + jnp.log(l), -1, -2)
    return o, lse


def _dilated_prompt(q, k, v, dil, nw):
    B, T, H, hd = q.shape
    L = T // dil
    nb = -(-L // nw)
    lp = nb * nw
    z = B * dil

    def by_residue(a):
        return a.reshape(B, L, dil, H, hd).transpose(0, 2, 1, 3, 4).reshape(z, L, H, hd)

    qs = jnp.pad(by_residue(q), ((0, 0), (0, lp - L), (0, 0), (0, 0))).reshape(z, nb, nw, H, hd)

    def key_blocks(a):
        ap = jnp.pad(by_residue(a), ((0, 0), (nw, lp - L), (0, 0), (0, 0)))
        return jnp.concatenate([ap[:, :lp].reshape(z, nb, nw, H, hd),
                                ap[:, nw:].reshape(z, nb, nw, H, hd)], axis=2)

    kb, vb = key_blocks(k), key_blocks(v)
    qi = jnp.arange(nw)[:, None]
    ks = jnp.arange(2 * nw)[None, :]
    dist = qi + nw - ks
    first = (jnp.arange(nb) * nw - nw)[:, None, None] + ks[None]
    mask = (dist >= 0) & (dist <= nw) & (first >= 0)
    s = jnp.einsum('znqhd,znkhd->znhqk', qs, kb) * (hd ** -0.5)
    s = jnp.where(mask[None, :, None], s, -jnp.inf)
    o, lse = _attend(s, vb, 'znhqk,znkhd->znqhd')
    o = o.reshape(z, lp, H, hd)[:, :L].reshape(B, dil, L, H, hd).transpose(0, 2, 1, 3, 4).reshape(B, T, H, hd)
    lse = lse.reshape(z, lp, H)[:, :L].reshape(B, dil, L, H).transpose(0, 2, 1, 3).reshape(B, T, H)
    return o, lse


def _dilated_sample(q, k, v, k_buf, v_buf, dil, nw):
    Tn = q.shape[1]
    lb = k_buf.shape[1]
    kc = jnp.concatenate([k_buf.astype(k.dtype), k], axis=1)
    vc = jnp.concatenate([v_buf.astype(v.dtype), v], axis=1)
    idx = lb + jnp.arange(Tn)[:, None] - dil * jnp.arange(nw + 1)[None, :]
    valid = idx >= 0
    idx = jnp.maximum(idx, 0)
    kg = jnp.take(kc, idx, axis=1)
    vg = jnp.take(vc, idx, axis=1)
    s = jnp.einsum('bqhd,bqkhd->bhqk', q, kg) * (q.shape[-1] ** -0.5)
    s = jnp.where(valid[None, None], s, -jnp.inf)
    o, lse = _attend(s, vg, 'bhqk,bqkhd->bqhd')
    return o, lse, kc[:, -lb:], vc[:, -lb:]


def _mixer_c_core(qkv, bufs, q_norm_g, k_norm_g):
    B, T, _ = qkv.shape
    qkv = qkv.reshape(B, T, C_GROUPS, 3, C_HEADS, C_HEAD_DIM)
    q = _rmsnorm(qkv[:, :, :, 0], q_norm_g[:, None, :])
    k = _rmsnorm(qkv[:, :, :, 1], k_norm_g[:, None, :])
    v = qkv[:, :, :, 2]
    outs, lses, new_bufs = [], [], []
    for gi in range(C_GROUPS):
        dil = C_DILATIONS[gi]
        nw = C_WINDOWS[gi] // dil
        qg, kg, vg = q[:, :, gi], k[:, :, gi], v[:, :, gi]
        if bufs is None:
            o, lse = _dilated_prompt(qg, kg, vg, dil, nw)
            lb = min(C_WINDOWS[gi], T)
            nk, nv = kg[:, T - lb:], vg[:, T - lb:]
        else:
            o, lse, nk, nv = _dilated_sample(qg, kg, vg, bufs[gi][0], bufs[gi][1], dil, nw)
        outs.append(o)
        lses.append(lse)
        new_bufs.append((nk, nv))
    wts = jax.nn.softmax(jnp.stack(lses), axis=0)
    o = jnp.sum(wts[..., None] * jnp.stack(outs), axis=0)
    return o.reshape(B * T, D_MODEL), new_bufs


def _mixer_c_prompt(h, B, T, w_qkv, slot, q_norm_g, k_norm_g, tm):
    W = D_MODEL
    outs, lses, new_bufs = [], [], []
    for gi in range(C_GROUPS):
        dil = C_DILATIONS[gi]
        nw = C_WINDOWS[gi] // dil
        L = T // dil
        assert L % nw == 0

        def to_res(a, w, dil=dil, L=L):
            return a.reshape(B, L, dil, w).transpose(0, 2, 1, 3).reshape(B * T, w)

        def from_res(a, w, dil=dil, L=L):
            return a.reshape(B, dil, L, w).transpose(0, 2, 1, 3).reshape(B, T, w)

        hp = h if dil == 1 else to_res(h, W)
        qkv = _linear(hp, w_qkv, (slot,), n_out=3 * W, tm=tm, tn=1024, col_off=gi * 3 * W // 1024)
        o, kn, lse = _band_attn(qkv, q_norm_g[gi], k_norm_g[gi], z=B * dil, L=L,
                                heads=C_HEADS, hd=C_HEAD_DIM, nw=nw)
        lb = min(C_WINDOWS[gi], T)
        outs.append(from_res(o, W).reshape(B, T, C_HEADS, C_HEAD_DIM))
        lses.append(from_res(lse, LANE)[..., :C_HEADS])
        nk = from_res(kn, W)[:, T - lb:].reshape(B, lb, C_HEADS, C_HEAD_DIM)
        nv = from_res(qkv[:, 2 * W:], W)[:, T - lb:].reshape(B, lb, C_HEADS, C_HEAD_DIM)
        new_bufs.append((nk, nv))
    wts = jax.nn.softmax(jnp.stack(lses), axis=0)
    o = jnp.sum(wts[..., None] * jnp.stack(outs), axis=0)
    return o.reshape(B * T, D_MODEL), new_bufs


def _route(logits, tm):
    M = logits.shape[0]
    top_v, top_i = lax.top_k(logits, TOP_K)
    gates = jax.nn.softmax(top_v, axis=-1)
    e_flat = top_i.reshape(-1)
    onehot = (e_flat[:, None] == jnp.arange(N_EXPERTS)[None, :]).astype(jnp.int32)
    counts = jnp.sum(onehot, axis=0)
    rank = jnp.sum((jnp.cumsum(onehot, axis=0) - onehot) * onehot, axis=1)
    padded = ((counts + tm - 1) // tm) * tm
    ends = jnp.cumsum(padded)
    pos = (ends - padded)[e_flat] + rank
    nt = (TOP_K * M + N_EXPERTS * (tm - 1)) // tm
    src = jnp.zeros((nt * tm,), jnp.int32).at[pos].set(jnp.arange(TOP_K * M, dtype=jnp.int32) // TOP_K)
    starts = jnp.arange(nt, dtype=jnp.int32) * tm
    tile_expert = jnp.sum((starts[:, None] >= ends[None, :]).astype(jnp.int32), axis=1)
    tile_valid = (starts < ends[-1]).astype(jnp.int32)
    last_expert = tile_expert[ends[-1] // tm - 1]
    tile_expert = jnp.where(tile_valid > 0, tile_expert, last_expert).astype(jnp.int32)
    return gates, pos.reshape(M, TOP_K), src, tile_expert, tile_valid


COMBINE_ROWS = 64


def _k_moe_combine(pos_ref, g_ref, ys_hbm, o_ref, buf_ref, sem_ref, *, rows):
    s = pl.program_id(0)
    slot = s % 2

    def start_step(step, into):
        for i in range(TOP_K * rows):
            src = pos_ref[step * (TOP_K * rows) + i]
            dst = (i % TOP_K) * rows + i // TOP_K
            pltpu.make_async_copy(ys_hbm.at[pl.ds(src, 1), :], buf_ref.at[into, pl.ds(dst, 1), :],
                                  sem_ref.at[into]).start(priority=i % 2)

    @pl.when(s == 0)
    def _():
        start_step(0, 0)

    @pl.when(s + 1 < pl.num_programs(0))
    def _():
        start_step(s + 1, 1 - slot)

    for i in range(TOP_K * rows):
        pltpu.make_async_copy(ys_hbm.at[pl.ds(0, 1), :], buf_ref.at[slot, pl.ds(i, 1), :],
                              sem_ref.at[slot]).wait()
    o_ref[...] = (g_ref[:, 0:1] * buf_ref[slot, 0:rows, :]
                  + g_ref[:, 1:2] * buf_ref[slot, rows:TOP_K * rows, :])


def _moe_combine(ys, pos, gates):
    M = pos.shape[0]
    D = ys.shape[1]
    rows = COMBINE_ROWS
    assert M % rows == 0 and TOP_K == 2
    return pl.pallas_call(
        functools.partial(_k_moe_combine, rows=rows),
        grid_spec=pltpu.PrefetchScalarGridSpec(
            num_scalar_prefetch=1,
            grid=(M // rows,),
            in_specs=[pl.BlockSpec((rows, TOP_K), lambda i, pos: (i, 0)),
                      pl.BlockSpec(memory_space=pl.ANY)],
            out_specs=pl.BlockSpec((rows, D), lambda i, pos: (i, 0)),
            scratch_shapes=[pltpu.VMEM((2, TOP_K * rows, D), F32), pltpu.SemaphoreType.DMA((2,))]),
        out_shape=jax.ShapeDtypeStruct((M, D), F32),
        compiler_params=_params(1),
        name="moe_combine",
    )(pos.reshape(-1).astype(jnp.int32), gates, ys)


def _moe_ffn(requests, p, layer, *, tm, tm_down):
    sizes = [h.shape[0] for h, _ in requests]
    h = jnp.concatenate([h for h, _ in requests], axis=0)
    logits = jnp.concatenate([lg for _, lg in requests], axis=0)
    gates, pos, src, tile_expert, tile_valid = _route(logits, tm)
    xs = jnp.take(h, src, axis=0)
    act = _moe_linear(xs, p['m_w_gate_up'], layer, tile_expert, tile_valid,
                      n_out=F_EXPERT, tm=tm, tn=512, up=True)
    split = tm // tm_down
    ys = _moe_linear(act, p['m_w_down'], layer, jnp.repeat(tile_expert, split),
                     jnp.repeat(tile_valid, split), n_out=D_MODEL, tm=tm_down, tn=512, up=False)
    y = _moe_combine(ys, pos, gates)
    out, start = [], 0
    for m in sizes:
        out.append(y[start:start + m])
        start += m
    return out


def _run_trunks(trunks, p, *, tm_moe, tm_moe_down):
    reqs = [next(t) for t in trunks]
    for layer in range(DEPTH // 2):
        ys = _moe_ffn(reqs, p, layer, tm=tm_moe, tm_down=tm_moe_down)
        nxt = []
        for t, y in zip(trunks, ys):
            try:
                nxt.append(t.send(y))
            except StopIteration as done:
                nxt.append(done.value)
        reqs = nxt
    return reqs


def _trunk(x, mod, st, p, *, tm):
    B, T, _ = x.shape
    M = B * T
    per_row = tm > T

    def vec(v):
        if per_row:
            return jnp.repeat(v, T, axis=0).reshape(1, M, D_MODEL)
        return v.reshape(B, 1, D_MODEL)

    tm_big_k = min(tm, 512)
    x = x.reshape(M, D_MODEL)
    a_conv, a_rec, b_re, b_im = [], [], [], []
    c_k = [[] for _ in range(C_GROUPS)]
    c_v = [[] for _ in range(C_GROUPS)]
    for i in range(DEPTH):
        sh1, sc1, gt1, sh2, sc2, gt2 = (vec(m) for m in jnp.split(mod[i], N_MOD, axis=-1))
        kind, slot = i % N_MIXERS, i // N_MIXERS
        if kind == 0:
            h = _modulate(x, p['g_mix'][i], sh1, sc1, tm=tm)[0]
            main = _linear(h, p['a_w_in'], (slot,), n_out=A_CONV_DIM + A_VAL_DIM, tm=tm, tn=1024)
            w_small = jnp.pad(p['a_w_in'][slot, :, A_CONV_DIM + A_VAL_DIM:],
                              ((0, 0), (0, LANE - 2 * A_V_HEADS)))
            small = _linear(h, w_small[None], (0,), n_out=LANE, tm=tm, tn=LANE)
            if st is None:
                buf0 = jnp.zeros((B, A_CONV - 1, A_CONV_DIM), F32)
                s0 = jnp.zeros((B, A_V_HEADS, A_HEAD_DIM, A_HEAD_DIM), F32)
            else:
                buf0, s0 = st['a_conv'][slot], st['a_rec'][slot]
            o, nbuf, ns = _mixer_a_core(main, small, buf0, s0, p['a_conv_w'][slot], p['a_log'][slot],
                                        p['a_dt_bias'][slot], p['a_norm_g'][slot], B, T)
            a_conv.append(nbuf)
            a_rec.append(ns)
            x = _linear(o, p['a_w_out'], (slot,), n_out=D_MODEL, tm=tm_big_k, tn=512,
                        mode="resid", res=x, gate=gt1)
        elif kind == 1:
            h, hf = _modulate(x, p['g_mix'][i], sh1, sc1, tm=tm, extra="f32")
            if st is None:
                r0 = jnp.zeros((B, B_GROUPS, B_STATE), F32)
                m0 = jnp.zeros((B, B_GROUPS, B_STATE), F32)
            else:
                r0, m0 = st['b_re'][slot], st['b_im'][slot]
            y, nr, ni = _s5_core(hf.reshape(B, T, D_MODEL), r0, m0, p['b_lambda_re'][slot],
                                 p['b_lambda_im'][slot], p['b_log_dt'][slot], p['b_B_re'][slot],
                                 p['b_B_im'][slot], p['b_C_re'][slot], p['b_C_im'][slot], p['b_D'][slot],
                                 tc=min(T, 32))
            b_re.append(nr)
            b_im.append(ni)
            x = _linear(y, p['b_w_glu'], (slot,), n_out=D_MODEL, tm=tm, tn=512,
                        mode="glu_resid", col_off2=D_MODEL // 512, res=x, gate=gt1)
        else:
            h = _modulate(x, p['g_mix'][i], sh1, sc1, tm=tm)[0]
            if st is None:
                o, nbs = _mixer_c_prompt(h, B, T, p['c_w_qkv'], slot, p['c_q_norm_g'][slot],
                                         p['c_k_norm_g'][slot], tm)
            else:
                qkv = _linear(h, p['c_w_qkv'], (slot,), n_out=C_GROUPS * 3 * D_MODEL, tm=tm, tn=512)
                bufs = [(st['c_k'][g][slot], st['c_v'][g][slot]) for g in range(C_GROUPS)]
                o, nbs = _mixer_c_core(qkv.reshape(B, T, -1), bufs, p['c_q_norm_g'][slot],
                                       p['c_k_norm_g'][slot])
            for g in range(C_GROUPS):
                c_k[g].append(nbs[g][0])
                c_v[g].append(nbs[g][1])
            x = _linear(o.astype(BF16), p['c_w_out'], (slot,), n_out=D_MODEL, tm=tm, tn=512,
                        mode="resid", res=x, gate=gt1)
        if i % 2 == 0:
            h = _modulate(x, p['g_ffn'][i], sh2, sc2, tm=tm)[0]
            act = _linear(h, p['f_w_gate_up'], (i // 2,), n_out=F_DENSE, tm=tm, tn=512, mode="swiglu",
                          col_off2=F_DENSE // 512, out_dtype=BF16)
            x = _linear(act, p['f_w_down'], (i // 2,), n_out=D_MODEL, tm=tm_big_k, tn=512,
                        mode="resid", res=x, gate=gt2)
        else:
            router = jnp.pad(p['m_router'][i // 2], ((0, 0), (0, LANE - N_EXPERTS)))
            h, logits = _modulate(x, p['g_ffn'][i], sh2, sc2, tm=tm, extra="router", router=router)
            y = yield h, logits[:, :N_EXPERTS]
            gt2_rows = gt2.reshape(M, D_MODEL) if per_row else jnp.repeat(gt2.reshape(B, D_MODEL), T, axis=0)
            x = x + gt2_rows * y
    stk = lambda lst: jnp.stack(lst, axis=0)
    states = (stk(a_conv), stk(a_rec), stk(b_re), stk(b_im), stk(c_k[0]), stk(c_v[0]),
              stk(c_k[1]), stk(c_v[1]), stk(c_k[2]), stk(c_v[2]))
    return x.reshape(B, T, D_MODEL), states


def kernel(x_prompt, x_sample, c_prompt, c_sample, state_a_conv, state_a_rec, state_b_re, state_b_im, cache_c_k0, cache_c_v0, cache_c_k1, cache_c_v1, cache_c_k2, cache_c_v2, g_mix, g_ffn, w_mod, b_mod, a_w_in, a_conv_w, a_log, a_dt_bias, a_norm_g, a_w_out, b_lambda_re, b_lambda_im, b_log_dt, b_B_re, b_B_im, b_C_re, b_C_im, b_D, b_w_glu, c_w_qkv, c_q_norm_g, c_k_norm_g, c_w_out, f_w_gate_up, f_w_down, m_router, m_w_gate_up, m_w_down):
    p = dict(g_mix=g_mix, g_ffn=g_ffn,
             a_w_in=a_w_in, a_conv_w=a_conv_w, a_log=a_log, a_dt_bias=a_dt_bias, a_norm_g=a_norm_g, a_w_out=a_w_out,
             b_lambda_re=b_lambda_re, b_lambda_im=b_lambda_im, b_log_dt=b_log_dt, b_B_re=b_B_re, b_B_im=b_B_im,
             b_C_re=b_C_re, b_C_im=b_C_im, b_D=b_D, b_w_glu=b_w_glu,
             c_w_qkv=c_w_qkv, c_q_norm_g=c_q_norm_g, c_k_norm_g=c_k_norm_g, c_w_out=c_w_out,
             f_w_gate_up=f_w_gate_up, f_w_down=f_w_down,
             m_router=m_router, m_w_gate_up=m_w_gate_up, m_w_down=m_w_down)
    st = dict(a_conv=state_a_conv, a_rec=state_a_rec, b_re=state_b_re, b_im=state_b_im,
              c_k=(cache_c_k0, cache_c_k1, cache_c_k2), c_v=(cache_c_v0, cache_c_v1, cache_c_v2))
    nb_p, nb_s = c_prompt.shape[0], c_sample.shape[0]
    c_all = jnp.concatenate([c_prompt, c_sample, jnp.zeros((16 - nb_p - nb_s, D_MODEL), F32)], axis=0)
    mod = _adaln(c_all, w_mod, b_mod)
    trunks = [_trunk(x_prompt, mod[:, :nb_p], None, p, tm=1024),
              _trunk(x_sample, mod[:, nb_p:nb_p + nb_s], st, p, tm=64)]
    (y_prompt, ps), (y_sample, ss) = _run_trunks(trunks, p, tm_moe=512, tm_moe_down=256)
    return (y_prompt, y_sample) + ps + ss
```
